```python
import math
import jax, jax.numpy as jnp
from jax import lax
import numpy as np

D_MODEL = 1024
BATCH = 8
SEQ = 2048
DEPTH = 1
DEC_BATCH = 32
DEC_SEQ = 4
PAST_LEN = 8192
PAGE_SIZE = 128

H_A = 8
HD_A = 64
DILATED_BRANCHES = ((128, 1), (512, 4), (2048, 16))
WIN_MAX = 2048
BLK = 128
ROT_DIM = HD_A // 4
ROPE_THETA = 500000.0
H_B = 4
DK_B = 64
DV_B = 128
GATE_RANK = 16
GATE_NORM = 16.0
GLA_CHUNK = 64
MIX_W = H_A * HD_A + H_B * DV_B
D_FF = 2816
D_PLE = 256
EPS = 1e-6
PROJ_SPLITS = (H_A * HD_A, H_A * HD_A, H_A * HD_A, H_B * DK_B, H_B * DK_B, H_B * DV_B, GATE_RANK, H_B * DV_B)
PROJ_W = sum(PROJ_SPLITS)

kernel_name = 'hymba_dilated_swa_gla_macaron_step'


def rmsnorm(x, g):
    xf = x.astype(jnp.float32)
    y = xf * lax.rsqrt(jnp.mean(xf * xf, axis=-1, keepdims=True) + EPS)
    return (y * g.astype(jnp.float32)).astype(x.dtype)


def swiglu(x, w_gu, w_down):
    gate, up = jnp.split(x @ w_gu, 2, axis=-1)
    return (jax.nn.silu(gate) * up) @ w_down


def partial_rope(x, pos):
    half = ROT_DIM // 2
    inv_freq = ROPE_THETA ** (-jnp.arange(half, dtype=jnp.float32) * (2.0 / ROT_DIM))
    ang = pos.astype(jnp.float32)[:, None] * inv_freq[None, :]
    cos = jnp.cos(ang)[None, :, None, :]
    sin = jnp.sin(ang)[None, :, None, :]
    xr = x[..., :ROT_DIM].astype(jnp.float32)
    x1, x2 = xr[..., :half], xr[..., half:]
    rot = jnp.concatenate([x1 * cos - x2 * sin, x2 * cos + x1 * sin], axis=-1)
    return jnp.concatenate([rot.astype(x.dtype), x[..., ROT_DIM:]], axis=-1)


def dilated_branch_prompt(q, k, v, window, dil):
    B, S, H, Dh = q.shape
    n_back = window // dil
    L = S // dil
    nb = -(-L // BLK)
    Lp = nb * BLK

    def to_sub(t):
        t = t.reshape(B, L, dil, H, Dh)
        return jnp.pad(t, ((0, 0), (0, Lp - L), (0, 0), (0, 0), (0, 0)))

    def key_blocks(t):
        tp = jnp.pad(to_sub(t), ((0, 0), (BLK, 0), (0, 0), (0, 0), (0, 0)))
        prev = tp[:, :Lp].reshape(B, nb, BLK, dil, H, Dh)
        cur = tp[:, BLK:].reshape(B, nb, BLK, dil, H, Dh)
        return jnp.concatenate([prev, cur], axis=2)

    qs = to_sub(q).reshape(B, nb, BLK, dil, H, Dh)
    kb, vb = key_blocks(k), key_blocks(v)
    s = jnp.einsum('bnqrhd,bnkrhd->bnrhqk', qs, kb) * (Dh ** -0.5)
    qi = jnp.arange(BLK)[:, None]
    ki = jnp.arange(2 * BLK)[None, :]
    dist = qi + BLK - ki
    band = (dist >= 0) & (dist <= n_back)
    key_sub = jnp.arange(nb)[:, None] * BLK - BLK + ki
    mask = band[None] & (key_sub >= 0)[:, None, :]
    s = jnp.where(mask[None, :, None, None], s, -jnp.inf)
    m = jnp.max(s, axis=-1, keepdims=True)
    p = jnp.exp(s - m)
    den = jnp.sum(p, axis=-1)
    o = jnp.einsum('bnrhqk,bnkrhd->bnqrhd', p, vb)
    den_t = jnp.transpose(den, (0, 1, 4, 2, 3))
    o = o / den_t[..., None]
    lse = jnp.transpose(m[..., 0], (0, 1, 4, 2, 3)) + jnp.log(den_t)
    o = o.reshape(B, Lp, dil, H, Dh)[:, :L].reshape(B, S, H, Dh)
    lse = lse.reshape(B, Lp, dil, H)[:, :L].reshape(B, S, H)
    return o, lse


def dilated_branch_sample(q, k_all, v_all, window, dil):
    B, T, H, Dh = q.shape
    n_back = window // dil
    buf = k_all.shape[1] - T
    idx = buf + jnp.arange(T)[:, None] - dil * jnp.arange(n_back + 1)[None, :]
    valid = idx >= 0
    idx = jnp.maximum(idx, 0)
    kg = jnp.take(k_all, idx, axis=1)
    vg = jnp.take(v_all, idx, axis=1)
    s = jnp.einsum('bthd,btjhd->bthj', q, kg) * (Dh ** -0.5)
    s = jnp.where(valid[None, :, None, :], s, -jnp.inf)
    m = jnp.max(s, axis=-1, keepdims=True)
    p = jnp.exp(s - m)
    den = jnp.sum(p, axis=-1)
    o = jnp.einsum('bthj,btjhd->bthd', p, vg) / den[..., None]
    lse = m[..., 0] + jnp.log(den)
    return o, lse


def combine_branches(outs, lses):
    w = jax.nn.softmax(jnp.stack(lses, axis=0), axis=0)
    return jnp.sum(w[..., None] * jnp.stack(outs, axis=0), axis=0)


def gla_chunked(q, k, v, gk, s0, chunk):
    B, T, H, dk = q.shape
    dv = v.shape[-1]
    n = T // chunk

    def blocks(t):
        return jnp.swapaxes(t.reshape(B, n, chunk, H, t.shape[-1]), 0, 1)

    causal = jnp.tril(jnp.ones((chunk, chunk), dtype=bool))

    def step(S, inp):
        qc, kc, vc, gc = inp
        bcum = jnp.cumsum(gc, axis=1)
        o_inter = jnp.einsum('bchk,bhkv->bchv', qc * jnp.exp(bcum), S)
        diff = bcum[:, :, None] - bcum[:, None, :]
        decay = jnp.exp(jnp.where(causal[None, :, :, None, None], diff, -jnp.inf))
        A = jnp.einsum('bihk,bjhk,bijhk->bhij', qc, kc, decay)
        o_intra = jnp.einsum('bhij,bjhv->bihv', A, vc)
        btot = bcum[:, -1]
        kdec = kc * jnp.exp(btot[:, None] - bcum)
        S_new = jnp.exp(btot)[..., None] * S + jnp.einsum('bchk,bchv->bhkv', kdec, vc)
        return S_new, o_inter + o_intra

    s_fin, o = lax.scan(step, s0, (blocks(q), blocks(k), blocks(v), blocks(gk)))
    o = jnp.swapaxes(o, 0, 1).reshape(B, T, H, dv)
    return o, s_fin


def parallel_mixer(xn, pos, k_buf, v_buf, s_gla, w_in, w_gla_a2, b_gla_a, g_gla_out, w_out):
    B, T, _ = xn.shape
    f32 = jnp.float32
    offs = [int(o) for o in np.cumsum(PROJ_SPLITS)[:-1]]
    qa, ka, va, qb, kb, vb, a1, gb = jnp.split(xn @ w_in, offs, axis=-1)
    qa = partial_rope(qa.reshape(B, T, H_A, HD_A), pos)
    ka = partial_rope(ka.reshape(B, T, H_A, HD_A), pos)
    va = va.reshape(B, T, H_A, HD_A)
    outs, lses = [], []
    if k_buf is None:
        for window, dil in DILATED_BRANCHES:
            o, l = dilated_branch_prompt(qa.astype(f32), ka.astype(f32), va.astype(f32), window, dil)
            outs.append(o)
            lses.append(l)
        keep = min(WIN_MAX, T)
        new_k, new_v = ka[:, T - keep:], va[:, T - keep:]
        s0 = jnp.zeros((B, H_B, DK_B, DV_B), f32)
    else:
        k_all = jnp.concatenate([k_buf.astype(ka.dtype), ka], axis=1)
        v_all = jnp.concatenate([v_buf.astype(va.dtype), va], axis=1)
        for window, dil in DILATED_BRANCHES:
            o, l = dilated_branch_sample(qa.astype(f32), k_all.astype(f32), v_all.astype(f32), window, dil)
            outs.append(o)
            lses.append(l)
        keep = k_buf.shape[1]
        new_k, new_v = k_all[:, T:T + keep], v_all[:, T:T + keep]
        s0 = s_gla.astype(f32)
    att = combine_branches(outs, lses).astype(xn.dtype).reshape(B, T, H_A * HD_A)
    qg = qb.reshape(B, T, H_B, DK_B).astype(f32) * (DK_B ** -0.5)
    kg = kb.reshape(B, T, H_B, DK_B).astype(f32)
    vg = vb.reshape(B, T, H_B, DV_B).astype(f32)
    gk = jax.nn.log_sigmoid((a1 @ w_gla_a2 + b_gla_a).astype(f32)) / GATE_NORM
    gk = gk.reshape(B, T, H_B, DK_B)
    o_gla, s_fin = gla_chunked(qg, kg, vg, gk, s0, math.gcd(T, GLA_CHUNK))
    o_gla = rmsnorm(o_gla, g_gla_out) * jax.nn.silu(gb.reshape(B, T, H_B, DV_B).astype(f32))
    o_gla = o_gla.astype(xn.dtype).reshape(B, T, H_B * DV_B)
    out = jnp.concatenate([att, o_gla], axis=-1) @ w_out
    return out, new_k, new_v, s_fin.astype(xn.dtype)


def decoder_layer(h, p, pos, k_buf, v_buf, s_gla, g_ffn1, w_ffn1_gu, w_ffn1_down, g_mix, w_in,
                  w_gla_a2, b_gla_a, g_gla_out, w_out, g_ffn2, w_ffn2_gu, w_ffn2_down,
                  g_ple, w_ple_gate, w_ple_proj):
    h = h + 0.5 * swiglu(rmsnorm(h, g_ffn1), w_ffn1_gu, w_ffn1_down)
    m, nk, nv, ns = parallel_mixer(rmsnorm(h, g_mix), pos, k_buf, v_buf, s_gla,
                                   w_in, w_gla_a2, b_gla_a, g_gla_out, w_out)
    h = h + m
    h = h + 0.5 * swiglu(rmsnorm(h, g_ffn2), w_ffn2_gu, w_ffn2_down)
    h = h + jax.nn.sigmoid(rmsnorm(h, g_ple) @ w_ple_gate) * (p @ w_ple_proj)
    return h, nk, nv, ns


def setup_inputs(seed: int = 0) -> dict:
    key = jax.random.key(seed)
    ks = jax.random.split(key, 32)
    win_buf = min(WIN_MAX, PAST_LEN)

    def nrm(k, shape, scale):
        return jax.random.normal(k, shape, jnp.float32) * scale

    def gain(k, shape):
        return 1.0 + nrm(k, shape, 0.01)

    return {
        'x_prompt': nrm(ks[0], (BATCH, SEQ, D_MODEL), 1.0),
        'x_sample': nrm(ks[1], (DEC_BATCH, DEC_SEQ, D_MODEL), 1.0),
        'cache_k_win': nrm(ks[2], (DEPTH, DEC_BATCH, win_buf, H_A, HD_A), 1.0),
        'cache_v_win': nrm(ks[3], (DEPTH, DEC_BATCH, win_buf, H_A, HD_A), 1.0),
        'state_gla': nrm(ks[4], (DEPTH, DEC_BATCH, H_B, DK_B, DV_B), 0.5),
        'p_prompt': nrm(ks[5], (DEPTH, BATCH, SEQ, D_PLE), 1.0),
        'p_sample': nrm(ks[6], (DEPTH, DEC_BATCH, DEC_SEQ, D_PLE), 1.0),
        'g_ffn1': gain(ks[7], (DEPTH, D_MODEL)),
        'w_ffn1_gu': nrm(ks[8], (DEPTH, D_MODEL, 2 * D_FF), D_MODEL ** -0.5),
        'w_ffn1_down': nrm(ks[9], (DEPTH, D_FF, D_MODEL), D_FF ** -0.5),
        'g_mix': gain(ks[10], (DEPTH, D_MODEL)),
        'w_in': nrm(ks[11], (DEPTH, D_MODEL, PROJ_W), D_MODEL ** -0.5),
        'w_gla_a2': nrm(ks[12], (DEPTH, GATE_RANK, H_B * DK_B), GATE_RANK ** -0.5),
        'b_gla_a': nrm(ks[13], (DEPTH, H_B * DK_B), 0.01),
        'g_gla_out': gain(ks[14], (DEPTH, DV_B)),
        'w_out': nrm(ks[15], (DEPTH, MIX_W, D_MODEL), MIX_W ** -0.5),
        'g_ffn2': gain(ks[16], (DEPTH, D_MODEL)),
        'w_ffn2_gu': nrm(ks[17], (DEPTH, D_MODEL, 2 * D_FF), D_MODEL ** -0.5),
        'w_ffn2_down': nrm(ks[18], (DEPTH, D_FF, D_MODEL), D_FF ** -0.5),
        'g_ple': gain(ks[19], (DEPTH, D_MODEL)),
        'w_ple_gate': nrm(ks[20], (DEPTH, D_MODEL, D_MODEL), D_MODEL ** -0.5),
        'w_ple_proj': nrm(ks[21], (DEPTH, D_PLE, D_MODEL), D_PLE ** -0.5),
        'g_final': gain(ks[22], (D_MODEL,)),
    }


def reference(x_prompt, x_sample, cache_k_win, cache_v_win, state_gla, p_prompt, p_sample,
              g_ffn1, w_ffn1_gu, w_ffn1_down, g_mix, w_in, w_gla_a2, b_gla_a, g_gla_out, w_out,
              g_ffn2, w_ffn2_gu, w_ffn2_down, g_ple, w_ple_gate, w_ple_proj, g_final):
    pos_p = jnp.arange(x_prompt.shape[1], dtype=jnp.int32)
    pos_s = PAST_LEN + jnp.arange(x_sample.shape[1], dtype=jnp.int32)
    hp, hs = x_prompt, x_sample
    kp_l, vp_l, sp_l, ks_l, vs_l, ss_l = [], [], [], [], [], []
    for i in range(DEPTH):
        lw = (g_ffn1[i], w_ffn1_gu[i], w_ffn1_down[i], g_mix[i], w_in[i], w_gla_a2[i], b_gla_a[i],
              g_gla_out[i], w_out[i], g_ffn2[i], w_ffn2_gu[i], w_ffn2_down[i], g_ple[i],
              w_ple_gate[i], w_ple_proj[i])
        hp, nk, nv, ns = decoder_layer(hp, p_prompt[i], pos_p, None, None, None, *lw)
        kp_l.append(nk)
        vp_l.append(nv)
        sp_l.append(ns)
        hs, nk, nv, ns = decoder_layer(hs, p_sample[i], pos_s, cache_k_win[i], cache_v_win[i],
                                       state_gla[i], *lw)
        ks_l.append(nk)
        vs_l.append(nv)
        ss_l.append(ns)
    y_prompt = rmsnorm(hp, g_final)
    y_sample = rmsnorm(hs, g_final)
    k_win_prompt = jnp.stack(kp_l, axis=0)
    v_win_prompt = jnp.stack(vp_l, axis=0)
    state_gla_prompt = jnp.stack(sp_l, axis=0)
    k_win_sample = jnp.stack(ks_l, axis=0)
    v_win_sample = jnp.stack(vs_l, axis=0)
    state_gla_sample = jnp.stack(ss_l, axis=0)
    return (y_prompt, y_sample, k_win_prompt, v_win_prompt, state_gla_prompt, k_win_sample, v_win_sample, state_gla_sample)
```

```python
import functools
import math

import jax
import jax.numpy as jnp
import numpy as np
from jax import lax
from jax.experimental import pallas as pl
from jax.experimental.pallas import tpu as pltpu

F32 = jnp.float32
BF16 = jnp.bfloat16

H_A = 8
HD_A = 64
ATT_W = H_A * HD_A
DILATED_BRANCHES = ((128, 1), (512, 4), (2048, 16))
ROT_DIM = HD_A // 4
ROPE_THETA = 500000.0
PAST_LEN = 8192
H_B = 4
DK_B = 64
DV_B = 128
GK_W = H_B * DK_B
GV_W = H_B * DV_B
GATE_RANK = 16
GATE_NORM = 16.0
GLA_CHUNK = 64
EPS = 1e-6
NEG = -1e30

LANES = 128
VMEM_LIMIT_BYTES = 56 * 1024 * 1024
ATT_BLK = 128


def _rms(x, g):
    return x * lax.rsqrt(jnp.mean(x * x, axis=-1, keepdims=True) + EPS) * g


def _dot(a, b):
    return jnp.dot(a, b, preferred_element_type=F32)


def _dot_nt(a, b):
    return lax.dot_general(a, b, (((1,), (1,)), ((), ())), preferred_element_type=F32)


def _dot_tn(a, b):
    return lax.dot_general(a, b, (((0,), (0,)), ((), ())), preferred_element_type=F32)


def _swiglu_half_step(x, g_ref, wgu_ref, wdn_ref):
    d_ff = wdn_ref.shape[0]
    xn = _rms(x, g_ref[...]).astype(BF16)
    n_chunk = 2 if d_ff % (2 * LANES) == 0 else 1
    fc = d_ff // n_chunk
    acc = None
    for c in range(n_chunk):
        gate = _dot(xn, wgu_ref[:, c * fc:(c + 1) * fc])
        up = _dot(xn, wgu_ref[:, d_ff + c * fc:d_ff + (c + 1) * fc])
        act = (gate * jax.nn.sigmoid(gate) * up).astype(BF16)
        part = _dot(act, wdn_ref[c * fc:(c + 1) * fc, :])
        acc = part if acc is None else acc + part
    return x + 0.5 * acc


def _pre_mixer_kernel(x_ref, g1_ref, wgu_ref, wdn_ref, gm_ref, win_ref, wa2_ref, ba_ref,
                      rc_ref, rsa_ref, rsb_ref,
                      h1_ref, qa_ref, ka_ref, va_ref, qb_ref, kb_ref, vb_ref, gk_ref, gb_ref):
    h1 = _swiglu_half_step(x_ref[...], g1_ref, wgu_ref, wdn_ref)
    h1_ref[...] = h1
    xm = _rms(h1, gm_ref[...]).astype(BF16)

    def proj(lo, width):
        return _dot(xm, win_ref[:, lo:lo + width])

    rc, rsa, rsb = rc_ref[...], rsa_ref[...], rsb_ref[...]

    def rope_store(dst_ref, t, scale):
        for c in range(ATT_W // LANES):
            blk = t[:, c * LANES:(c + 1) * LANES]
            rot = (blk * rc + pltpu.roll(blk, ROT_DIM // 2, 1) * rsa
                   + pltpu.roll(blk, LANES - ROT_DIM // 2, 1) * rsb)
            dst_ref[:, c * LANES:(c + 1) * LANES] = rot * scale if scale != 1.0 else rot

    rope_store(qa_ref, proj(0, ATT_W), HD_A ** -0.5)
    rope_store(ka_ref, proj(ATT_W, ATT_W), 1.0)
    va_ref[...] = proj(2 * ATT_W, ATT_W)
    off = 3 * ATT_W
    qb_ref[...] = proj(off, GK_W) * (DK_B ** -0.5)
    kb_ref[...] = proj(off + GK_W, GK_W)
    vb_ref[...] = proj(off + 2 * GK_W, GV_W)
    gb_ref[...] = proj(off + 2 * GK_W + GV_W, GV_W)
    a1 = proj(off + 2 * GK_W + 2 * GV_W, LANES)
    z = _dot(a1.astype(BF16), wa2_ref[...]) + ba_ref[...]
    gk_ref[...] = (jnp.minimum(z, 0.0) - jnp.log1p(jnp.exp(-jnp.abs(z)))) * (1.0 / GATE_NORM)


def _pre_mixer(x, w, rope_tabs, tm, rope_period_blocks):
    n, d = x.shape
    grid = (n // tm,)

    def tok(width):
        return pl.BlockSpec((tm, width), lambda i: (i, 0))

    def full(a):
        return pl.BlockSpec(a.shape, lambda i: (0,) * a.ndim)

    rope_spec = pl.BlockSpec((tm, LANES), lambda i: (i % rope_period_blocks, 0))
    weights = (w["g_ffn1"], w["w_ffn1_gu"], w["w_ffn1_down"], w["g_mix"], w["w_in"],
               w["w_gla_a2"], w["b_gla_a"])
    out_widths = (d, ATT_W, ATT_W, ATT_W, GK_W, GK_W, GV_W, GK_W, GV_W)
    return pl.pallas_call(
        _pre_mixer_kernel,
        grid=grid,
        in_specs=[tok(d)] + [full(a) for a in weights] + [rope_spec] * 3,
        out_specs=[tok(wd) for wd in out_widths],
        out_shape=[jax.ShapeDtypeStruct((n, wd), F32) for wd in out_widths],
        compiler_params=pltpu.CompilerParams(
            dimension_semantics=("parallel",), vmem_limit_bytes=VMEM_LIMIT_BYTES),
        name="pre_mixer",
    )(x, *weights, *rope_tabs)


def _post_mixer_kernel(h1_ref, att_ref, gla_ref, p_ref, wout_ref, g2_ref, wgu_ref, wdn_ref,
                       gp_ref, wpg_ref, wpp_ref, gf_ref, y_ref):
    mix = _dot(att_ref[...], wout_ref[:ATT_W, :]) + _dot(gla_ref[...], wout_ref[ATT_W:, :])
    h2 = h1_ref[...] + mix
    h3 = _swiglu_half_step(h2, g2_ref, wgu_ref, wdn_ref)
    gate = jax.nn.sigmoid(_dot(_rms(h3, gp_ref[...]).astype(BF16), wpg_ref[...]))
    h4 = h3 + gate * _dot(p_ref[...].astype(BF16), wpp_ref[...])
    y_ref[...] = _rms(h4, gf_ref[...])


def _post_mixer(h1, att, gla, p, w, tm):
    n, d = h1.shape
    grid = (n // tm,)

    def tok(width):
        return pl.BlockSpec((tm, width), lambda i: (i, 0))

    def full(a):
        return pl.BlockSpec(a.shape, lambda i: (0,) * a.ndim)

    weights = (w["w_out"], w["g_ffn2"], w["w_ffn2_gu"], w["w_ffn2_down"], w["g_ple"],
               w["w_ple_gate"], w["w_ple_proj"], w["g_final"])
    return pl.pallas_call(
        _post_mixer_kernel,
        grid=grid,
        in_specs=[tok(d), tok(ATT_W), tok(GV_W), tok(p.shape[1])] + [full(a) for a in weights],
        out_specs=tok(d),
        out_shape=jax.ShapeDtypeStruct((n, d), F32),
        compiler_params=pltpu.CompilerParams(
            dimension_semantics=("parallel",), vmem_limit_bytes=VMEM_LIMIT_BYTES),
        name="post_mixer",
    )(h1, att, gla, p, *weights)


def _attn_prompt_kernel(q_ref, k_ref, v_ref, o_ref, oacc_ref, lse_ref):
    seq = q_ref.shape[1]
    qv, kv, vv = q_ref.at[0], k_ref.at[0], v_ref.at[0]
    blk = ATT_BLK
    lane = lax.broadcasted_iota(jnp.int32, (1, LANES), 1)
    head0 = lane < HD_A
    qi = lax.broadcasted_iota(jnp.int32, (blk, 2 * blk), 0)
    kj = lax.broadcasted_iota(jnp.int32, (blk, 2 * blk), 1)
    in_cur = (kj >= blk) & (kj - blk <= qi)
    in_prev = (kj < blk) & (qi <= kj)
    bias_first = jnp.where(in_cur, 0.0, NEG)
    bias_rest = jnp.where(in_cur | in_prev, 0.0, NEG)

    for bi, (window, dil) in enumerate(DILATED_BRANCHES):
        assert window // dil == blk
        nb = seq // (dil * blk)

        def body(idx, carry, dil=dil, nb=nb, bi=bi):
            r = idx // nb
            n = idx % nb
            start = r + n * (blk * dil)
            pstart = jnp.maximum(start - blk * dil, r)
            if dil == 1:
                start = pl.multiple_of(start, blk)
                pstart = pl.multiple_of(pstart, blk)

            def rows(ref, s0):
                if dil == 1:
                    return ref[pl.ds(s0, blk), :]
                return ref[pl.ds(s0, blk, stride=dil), :]

            qb = rows(qv, start)
            kcat = jnp.concatenate([rows(kv, pstart), rows(kv, start)], axis=0).astype(BF16)
            vcat = jnp.concatenate([rows(vv, pstart), rows(vv, start)], axis=0).astype(BF16)
            bias = jnp.where(n > 0, bias_rest, bias_first)
            outs, lses = [], []
            for hmask in (head0, ~head0):
                qh = jnp.where(hmask, qb, 0.0).astype(BF16)
                s = _dot_nt(qh, kcat) + bias
                m = jnp.max(s, axis=-1, keepdims=True)
                p = jnp.exp(s - m)
                den = jnp.sum(p, axis=-1, keepdims=True)
                outs.append(_dot(p.astype(BF16), vcat) / den)
                lses.append(m + jnp.log(den))
            o = jnp.where(head0, outs[0], outs[1])
            l = jnp.where(head0, lses[0], lses[1])
            if dil == 1:
                oacc_ref[bi, pl.ds(start, blk), :] = o
                lse_ref[bi, pl.ds(start, blk), :] = l
            else:
                oacc_ref.at[bi][pl.ds(start, blk, stride=dil), :] = o
                lse_ref.at[bi][pl.ds(start, blk, stride=dil), :] = l
            return carry

        lax.fori_loop(0, seq // blk, body, 0)

    rows_per = 256
    for c in range(seq // rows_per):
        sl = slice(c * rows_per, (c + 1) * rows_per)
        l0, l1, l2 = lse_ref[0, sl, :], lse_ref[1, sl, :], lse_ref[2, sl, :]
        mx = jnp.maximum(jnp.maximum(l0, l1), l2)
        w0, w1, w2 = jnp.exp(l0 - mx), jnp.exp(l1 - mx), jnp.exp(l2 - mx)
        tot = w0 + w1 + w2
        num = w0 * oacc_ref[0, sl, :] + w1 * oacc_ref[1, sl, :] + w2 * oacc_ref[2, sl, :]
        o_ref[0, sl, :] = (num / tot).astype(o_ref.dtype)


def _attn_prompt(q, k, v):
    b, seq, width = q.shape
    spec = pl.BlockSpec((1, seq, LANES), lambda i, j: (i, 0, j))
    nbr = len(DILATED_BRANCHES)
    return pl.pallas_call(
        _attn_prompt_kernel,
        grid=(b, width // LANES),
        in_specs=[spec] * 3,
        out_specs=spec,
        out_shape=jax.ShapeDtypeStruct((b, seq, width), BF16),
        scratch_shapes=[pltpu.VMEM((nbr, seq, LANES), F32), pltpu.VMEM((nbr, seq, LANES), F32)],
        compiler_params=pltpu.CompilerParams(
            dimension_semantics=("parallel", "parallel"), vmem_limit_bytes=VMEM_LIMIT_BYTES),
        name="attn_prompt",
    )(q, k, v)


SUB = 16


def _attn_sample_kernel(q_ref, kn_ref, vn_ref, kt_ref, vt_ref, kd_ref, vd_ref, o_ref):
    t_new = q_ref.shape[1]
    n_back = ATT_BLK
    tail_a = kt_ref.shape[1]
    tail_rows = tail_a * SUB
    assert t_new <= SUB and SUB % 4 == 0 and kd_ref.shape[1] == n_back
    assert tail_rows == n_back * 4
    kn, vn = kn_ref[0], vn_ref[0]
    un = lax.broadcasted_iota(jnp.int32, (t_new, 1, 1), 0)
    ai = lax.broadcasted_iota(jnp.int32, (n_back // SUB, SUB, 1, 1), 0)
    ci = lax.broadcasted_iota(jnp.int32, (n_back // SUB, SUB, 1, 1), 1)
    near_row = ai * SUB + ci

    def branch(qt, kc, vc, cmask, nmask):
        lead = tuple(range(kc.ndim - 2))
        sc = jnp.sum(kc * qt, axis=-1, keepdims=True)
        sn = jnp.sum(kn * qt, axis=-1, keepdims=True)
        if cmask is not None:
            sc = jnp.where(cmask, sc, NEG)
        sn = jnp.where(nmask, sn, NEG)
        m = jnp.maximum(jnp.max(sc, axis=lead), jnp.max(sn, axis=0))
        pc = jnp.exp(sc - m)
        pn = jnp.exp(sn - m)
        den = jnp.sum(pc, axis=lead) + jnp.sum(pn, axis=0)
        o = (jnp.sum(pc * vc, axis=lead) + jnp.sum(pn * vn, axis=0)) / den
        return o, m + jnp.log(den)

    for t in range(t_new):
        qt = q_ref[0, t]
        res = []
        a0 = tail_a - n_back // SUB
        res.append(branch(qt, kt_ref[0, a0:], vt_ref[0, a0:], near_row >= t, un <= t))
        k4 = jnp.stack([kt_ref[0, :, t + 4 * m] for m in range(SUB // 4)], axis=0)
        v4 = jnp.stack([vt_ref[0, :, t + 4 * m] for m in range(SUB // 4)], axis=0)
        res.append(branch(qt, k4, v4, None, un == t))
        res.append(branch(qt, kd_ref[0, :, t], vd_ref[0, :, t], None, un == t))
        mx = jnp.maximum(jnp.maximum(res[0][1], res[1][1]), res[2][1])
        ws = [jnp.exp(l - mx) for _, l in res]
        num = ws[0] * res[0][0] + ws[1] * res[1][0] + ws[2] * res[2][0]
        o_ref[0, t] = num / (ws[0] + ws[1] + ws[2])


def _attn_sample(q5, kn5, vn5, kc6, vc6):
    b, t_new, h, dh = q5.shape
    n_a = kc6.shape[1]
    assert tuple(d for _, d in DILATED_BRANCHES) == (1, 4, SUB) and n_a == ATT_BLK
    tail_a = ATT_BLK * 4 // SUB
    new_spec = pl.BlockSpec((1, t_new, h, dh), lambda i: (i, 0, 0, 0))
    tail_spec = pl.BlockSpec((1, tail_a, SUB, h, dh), lambda i: (i, n_a // tail_a - 1, 0, 0, 0))
    dil_spec = pl.BlockSpec((1, n_a, t_new, h, dh), lambda i: (i, 0, 0, 0, 0))
    return pl.pallas_call(
        _attn_sample_kernel,
        grid=(b,),
        in_specs=[new_spec] * 3 + [tail_spec, tail_spec, dil_spec, dil_spec],
        out_specs=new_spec,
        out_shape=jax.ShapeDtypeStruct((b, t_new, h, dh), F32),
        compiler_params=pltpu.CompilerParams(
            dimension_semantics=("parallel",), vmem_limit_bytes=VMEM_LIMIT_BYTES),
        name="attn_sample",
    )(q5, kn5, vn5, kc6, vc6, kc6, vc6)


def _cache_shift_kernel(kc_ref, vc_ref, kn_ref, vn_ref, ko_ref, vo_ref, sems):
    nb, rows = kc_ref.shape[0], kc_ref.shape[1]
    t_new = kn_ref.shape[1]
    copies = []
    for i, (src, new, dst) in enumerate(((kc_ref, kn_ref, ko_ref), (vc_ref, vn_ref, vo_ref))):
        for b in range(nb):
            copies.append(pltpu.make_async_copy(
                src.at[b, pl.ds(t_new, rows - t_new)], dst.at[b, pl.ds(0, rows - t_new)],
                sems.at[i * (nb + 1) + b]))
        copies.append(pltpu.make_async_copy(
            new, dst.at[:, pl.ds(rows - t_new, t_new)], sems.at[i * (nb + 1) + nb]))
    for cp in copies:
        cp.start()
    for cp in copies:
        cp.wait()


def _cache_shift(kc, vc, kn5, vn5):
    nb = kc.shape[0]
    any_spec = pl.BlockSpec(memory_space=pl.ANY)
    new_spec = pl.BlockSpec(kn5.shape, lambda i: (0, 0, 0, 0))
    return pl.pallas_call(
        _cache_shift_kernel,
        grid=(1,),
        in_specs=[any_spec, any_spec, new_spec, new_spec],
        out_specs=[any_spec, any_spec],
        out_shape=[jax.ShapeDtypeStruct(kc.shape, kc.dtype)] * 2,
        scratch_shapes=[pltpu.SemaphoreType.DMA((2 * (nb + 1),))],
        name="cache_shift",
    )(kc, vc, kn5, vn5)


def _gla_decay_matrix(chunk):
    t = np.arange(chunk)[:, None]
    u = np.arange(chunk)[None, :]
    blocks = [u <= t, u > t]
    s = chunk // 2
    while s >= 1:
        mid = (t // (2 * s)) * (2 * s) + s
        upper = (t >= mid) & (u >= mid) & (u <= t)
        lower = (t < mid) & (u > t) & (u < mid)
        blocks.append(upper | lower)
        s //= 2
    return np.concatenate(blocks, axis=0).astype(np.float32)


def _gla_kernel(q_ref, k_ref, v_ref, g_ref, gb_ref, s0_ref, m_ref, gout_ref,
                o_ref, sfin_ref, st_ref, *, chunk):
    j = pl.program_id(1)
    n_lvl = int(math.log2(chunk))
    tb = q_ref.shape[1]
    hk = lax.broadcasted_iota(jnp.int32, (1, GK_W), 1) // DK_B
    hv = lax.broadcasted_iota(jnp.int32, (1, GV_W), 1) // DV_B
    hv_rows = lax.broadcasted_iota(jnp.int32, (GV_W, 1), 0) // DV_B
    state_mask = hv_rows == hk

    @pl.when(j == 0)
    def _():
        s0t = s0_ref[0].reshape(GK_W, DV_B).T
        st_ref[...] = jnp.where(state_mask, jnp.concatenate([s0t] * H_B, axis=0), 0.0)

    pi = lax.broadcasted_iota(jnp.int32, (chunk, H_B * chunk), 0)
    pj = lax.broadcasted_iota(jnp.int32, (chunk, H_B * chunk), 1) % chunk
    pair_masks = []
    s = chunk // 2
    while s >= 1:
        same = (pi // (2 * s)) == (pj // (2 * s))
        pair_masks.append(same & (pi % (2 * s) >= s) & (pj % (2 * s) < s))
        s //= 2
    pair_masks.append(pi == pj)
    a_head = lax.broadcasted_iota(jnp.int32, (H_B * chunk, 1), 0) // chunk

    mdec = m_ref[...]
    gout = gout_ref[...]
    for c in range(tb // chunk):
        sl = slice(c * chunk, (c + 1) * chunk)
        q, k, v, g = q_ref[0, sl, :], k_ref[0, sl, :], v_ref[0, sl, :], g_ref[0, sl, :]
        g_hi = g.astype(BF16)
        r1 = g - g_hi.astype(F32)
        g_mid = r1.astype(BF16)
        g_lo = (r1 - g_mid.astype(F32)).astype(BF16)
        e = _dot(mdec, g_hi) + _dot(mdec, g_mid) + _dot(mdec, g_lo)
        x = jnp.exp(e)
        x_cum = x[0:chunk]
        x_rem = x[chunk:2 * chunk]
        a_all = jnp.zeros((chunk, H_B * chunk), F32)
        for lvl in range(n_lvl + 1):
            if lvl < n_lvl:
                xl = x[(2 + lvl) * chunk:(3 + lvl) * chunk]
                ql, kl = q * xl, k * xl
            else:
                ql, kl = q, k
            kl_bd = jnp.where(a_head == hk, jnp.concatenate([kl] * H_B, axis=0), 0.0)
            r = _dot_nt(ql.astype(BF16), kl_bd.astype(BF16))
            a_all = jnp.where(pair_masks[lvl], r, a_all)
        v_bd = jnp.where(a_head == hv, jnp.concatenate([v] * H_B, axis=0), 0.0)
        st = st_ref[...]
        o = (_dot_nt((q * x_cum).astype(BF16), st.astype(BF16))
             + _dot(a_all.astype(BF16), v_bd.astype(BF16)))
        upd = _dot_tn(v.astype(BF16), (k * x_rem).astype(BF16))
        st_ref[...] = st * x_cum[chunk - 1:chunk, :] + jnp.where(state_mask, upd, 0.0)
        gb = gb_ref[0, sl, :]
        for h in range(H_B):
            hs = slice(h * DV_B, (h + 1) * DV_B)
            gh = gb[:, hs]
            o_ref[0, sl, hs] = (_rms(o[:, hs], gout) * (gh * jax.nn.sigmoid(gh))).astype(o_ref.dtype)

    @pl.when(j == pl.num_programs(1) - 1)
    def _():
        st = st_ref[...]
        s_t = st[0:DV_B]
        for h in range(1, H_B):
            s_t = s_t + st[h * DV_B:(h + 1) * DV_B]
        sfin_ref[0] = s_t.T.reshape(H_B, DK_B, DV_B)


def _gla(q, k, v, g, gb, s0, g_out, chunk, tb):
    b, t, _ = q.shape
    mdec = jnp.asarray(_gla_decay_matrix(chunk), dtype=BF16)

    def tok(width):
        return pl.BlockSpec((1, tb, width), lambda i, j: (i, j, 0))

    state_spec = pl.BlockSpec((1, H_B, DK_B, DV_B), lambda i, j: (i, 0, 0, 0))
    return pl.pallas_call(
        functools.partial(_gla_kernel, chunk=chunk),
        grid=(b, t // tb),
        in_specs=[tok(GK_W), tok(GK_W), tok(GV_W), tok(GK_W), tok(GV_W), state_spec,
                  pl.BlockSpec(mdec.shape, lambda i, j: (0, 0)),
                  pl.BlockSpec(g_out.shape, lambda i, j: (0, 0))],
        out_specs=[tok(GV_W), state_spec],
        out_shape=[jax.ShapeDtypeStruct((b, t, GV_W), BF16),
                   jax.ShapeDtypeStruct((b, H_B, DK_B, DV_B), F32)],
        scratch_shapes=[pltpu.VMEM((GV_W, GK_W), F32)],
        compiler_params=pltpu.CompilerParams(
            dimension_semantics=("parallel", "arbitrary"), vmem_limit_bytes=VMEM_LIMIT_BYTES),
        name="gla",
    )(q, k, v, g, gb, s0, mdec, g_out)


def _rope_tables(pos):
    half = ROT_DIM // 2
    inv_freq = ROPE_THETA ** (-jnp.arange(half, dtype=F32) * (2.0 / ROT_DIM))
    ang = pos.astype(F32)[:, None] * inv_freq[None, :]
    cos, sin = jnp.cos(ang), jnp.sin(ang)
    n = pos.shape[0]
    ones = jnp.ones((n, HD_A - ROT_DIM), F32)
    zeros = jnp.zeros((n, HD_A - ROT_DIM), F32)
    zh = jnp.zeros((n, half), F32)
    c = jnp.concatenate([cos, cos, ones], axis=1)
    sa = jnp.concatenate([zh, sin, zeros], axis=1)
    sb = jnp.concatenate([-sin, zh, zeros], axis=1)
    rep = LANES // HD_A
    return tuple(jnp.tile(tab, (1, rep)) for tab in (c, sa, sb))


def _prep_weights(g_ffn1, w_ffn1_gu, w_ffn1_down, g_mix, w_in, w_gla_a2, b_gla_a, g_gla_out, w_out,
                  g_ffn2, w_ffn2_gu, w_ffn2_down, g_ple, w_ple_gate, w_ple_proj, g_final):
    a1_lo = 3 * ATT_W + 2 * GK_W + GV_W
    win = w_in[0]
    win = jnp.concatenate(
        [win[:, :a1_lo], win[:, a1_lo + GATE_RANK:], win[:, a1_lo:a1_lo + GATE_RANK],
         jnp.zeros((win.shape[0], LANES - GATE_RANK), win.dtype)], axis=1)
    wa2 = jnp.concatenate(
        [w_gla_a2[0], jnp.zeros((LANES - GATE_RANK, GK_W), w_gla_a2.dtype)], axis=0)
    return {
        "g_ffn1": g_ffn1, "w_ffn1_gu": w_ffn1_gu[0].astype(BF16),
        "w_ffn1_down": w_ffn1_down[0].astype(BF16), "g_mix": g_mix,
        "w_in": win.astype(BF16), "w_gla_a2": wa2.astype(BF16), "b_gla_a": b_gla_a,
        "g_gla_out": g_gla_out, "w_out": w_out[0].astype(BF16), "g_ffn2": g_ffn2,
        "w_ffn2_gu": w_ffn2_gu[0].astype(BF16), "w_ffn2_down": w_ffn2_down[0].astype(BF16),
        "g_ple": g_ple, "w_ple_gate": w_ple_gate[0].astype(BF16),
        "w_ple_proj": w_ple_proj[0].astype(BF16), "g_final": g_final.reshape(1, -1),
    }


def kernel(x_prompt, x_sample, cache_k_win, cache_v_win, state_gla, p_prompt, p_sample, g_ffn1, w_ffn1_gu, w_ffn1_down, g_mix, w_in, w_gla_a2, b_gla_a, g_gla_out, w_out, g_ffn2, w_ffn2_gu, w_ffn2_down, g_ple, w_ple_gate, w_ple_proj, g_final):
    assert w_in.shape[0] == 1, "single-layer step"
    w = _prep_weights(g_ffn1, w_ffn1_gu, w_ffn1_down, g_mix, w_in, w_gla_a2, b_gla_a, g_gla_out,
                      w_out, g_ffn2, w_ffn2_gu, w_ffn2_down, g_ple, w_ple_gate, w_ple_proj, g_final)
    bp, seq, d = x_prompt.shape
    bs, t_new, _ = x_sample.shape
    win_buf = cache_k_win.shape[2]
    assert seq == DILATED_BRANCHES[-1][0] == win_buf

    tm = 512
    rope_p = _rope_tables(jnp.arange(seq, dtype=jnp.int32))
    (h1, qa, ka, va, qb, kb, vb, gk, gb) = _pre_mixer(
        x_prompt.reshape(bp * seq, d), w, rope_p, tm, seq // tm)
    r3 = lambda a: a.reshape(bp, seq, a.shape[-1])
    att_p = _attn_prompt(r3(qa), r3(ka), r3(va))
    gla_p, state_p = _gla(r3(qb), r3(kb), r3(vb), r3(gk), r3(gb),
                          jnp.zeros((bp, H_B, DK_B, DV_B), F32), w["g_gla_out"],
                          chunk=GLA_CHUNK, tb=4 * GLA_CHUNK)
    y_prompt = _post_mixer(h1, att_p.reshape(bp * seq, ATT_W), gla_p.reshape(bp * seq, GV_W),
                           p_prompt[0].reshape(bp * seq, -1), w, tm).reshape(bp, seq, d)
    k_win_prompt = ka.reshape(1, bp, seq, H_A, HD_A)
    v_win_prompt = va.reshape(1, bp, seq, H_A, HD_A)

    ns = bs * t_new
    rope_s = _rope_tables(jnp.tile(PAST_LEN + jnp.arange(t_new, dtype=jnp.int32), bs))
    (h1s, qas, kas, vas, qbs, kbs, vbs, gks, gbs) = _pre_mixer(
        x_sample.reshape(ns, d), w, rope_s, ns, 1)
    r5 = lambda a: a.reshape(bs, t_new, H_A, HD_A)
    kn5, vn5 = r5(kas), r5(vas)
    kc, vc = cache_k_win[0], cache_v_win[0]
    c6 = lambda a: a.reshape(bs, win_buf // SUB, SUB, H_A, HD_A)
    att_s = _attn_sample(r5(qas), kn5, vn5, c6(kc), c6(vc))
    k_win_sample, v_win_sample = _cache_shift(kc, vc, kn5, vn5)
    s_chunk = 16
    pad = lambda a: jnp.pad(a.reshape(bs, t_new, a.shape[-1]), ((0, 0), (0, s_chunk - t_new), (0, 0)))
    gla_s, state_s = _gla(pad(qbs), pad(kbs), pad(vbs), pad(gks), pad(gbs), state_gla[0],
                          w["g_gla_out"], chunk=s_chunk, tb=s_chunk)
    y_sample = _post_mixer(h1s, att_s.reshape(ns, ATT_W).astype(BF16),
                           gla_s[:, :t_new].reshape(ns, GV_W), p_sample[0].reshape(ns, -1),
                           w, ns).reshape(bs, t_new, d)

    return (y_prompt, y_sample, k_win_prompt, v_win_prompt, state_p[None],
            k_win_sample[None], v_win_sample[None], state_s[None])
```

```python
import functools
import math

import jax
import jax.numpy as jnp
import numpy as np
from jax import lax
from jax.experimental import pallas as pl
from jax.experimental.pallas import tpu as pltpu

F32 = jnp.float32
BF16 = jnp.bfloat16

H_A = 8
HD_A = 64
ATT_W = H_A * HD_A
DILATED_BRANCHES = ((128, 1), (512, 4), (2048, 16))
ROT_DIM = HD_A // 4
ROPE_THETA = 500000.0
PAST_LEN = 8192
H_B = 4
DK_B = 64
DV_B = 128
GK_W = H_B * DK_B
GV_W = H_B * DV_B
GATE_RANK = 16
GATE_NORM = 16.0
GLA_CHUNK = 64
EPS = 1e-6
NEG = -1e30

LANES = 128
VMEM_LIMIT_BYTES = 56 * 1024 * 1024
ATT_BLK = 128
ATT_UNROLL = 8


def _rms(x, g):
    return x * lax.rsqrt(jnp.mean(x * x, axis=-1, keepdims=True) + EPS) * g


def _dot(a, b):
    return jnp.dot(a, b, preferred_element_type=F32)


def _dot_nt(a, b):
    return lax.dot_general(a, b, (((1,), (1,)), ((), ())), preferred_element_type=F32)


def _dot_tn(a, b):
    return lax.dot_general(a, b, (((0,), (0,)), ((), ())), preferred_element_type=F32)


def _swiglu_half_step(x, g_ref, wgu_ref, wdn_ref):
    d_ff = wdn_ref.shape[0]
    xn = _rms(x, g_ref[...]).astype(BF16)
    n_chunk = 2 if d_ff % (2 * LANES) == 0 else 1
    fc = d_ff // n_chunk
    acc = None
    for c in range(n_chunk):
        gate = _dot(xn, wgu_ref[:, c * fc:(c + 1) * fc])
        up = _dot(xn, wgu_ref[:, d_ff + c * fc:d_ff + (c + 1) * fc])
        act = (gate * jax.nn.sigmoid(gate) * up).astype(BF16)
        part = _dot(act, wdn_ref[c * fc:(c + 1) * fc, :])
        acc = part if acc is None else acc + part
    return x + 0.5 * acc


def _pre_mixer_kernel(x_ref, g1_ref, wgu_ref, wdn_ref, gm_ref, win_ref, wa2_ref, ba_ref,
                      rc_ref, rsa_ref, rsb_ref,
                      h1_ref, qa_ref, ka_ref, va_ref, qb_ref, kb_ref, vb_ref, gk_ref, gb_ref):
    h1 = _swiglu_half_step(x_ref[...], g1_ref, wgu_ref, wdn_ref)
    h1_ref[...] = h1
    xm = _rms(h1, gm_ref[...]).astype(BF16)

    def proj(lo, width):
        return _dot(xm, win_ref[:, lo:lo + width])

    rc, rsa, rsb = rc_ref[...], rsa_ref[...], rsb_ref[...]

    def rope_store(dst_ref, t, scale):
        for c in range(ATT_W // LANES):
            blk = t[:, c * LANES:(c + 1) * LANES]
            rot = (blk * rc + pltpu.roll(blk, ROT_DIM // 2, 1) * rsa
                   + pltpu.roll(blk, LANES - ROT_DIM // 2, 1) * rsb)
            dst_ref[:, c * LANES:(c + 1) * LANES] = rot * scale if scale != 1.0 else rot

    rope_store(qa_ref, proj(0, ATT_W), HD_A ** -0.5)
    rope_store(ka_ref, proj(ATT_W, ATT_W), 1.0)
    va_ref[...] = proj(2 * ATT_W, ATT_W)
    off = 3 * ATT_W
    qb_ref[...] = proj(off, GK_W) * (DK_B ** -0.5)
    kb_ref[...] = proj(off + GK_W, GK_W)
    vb_ref[...] = proj(off + 2 * GK_W, GV_W)
    gb_ref[...] = proj(off + 2 * GK_W + GV_W, GV_W)
    a1 = proj(off + 2 * GK_W + 2 * GV_W, LANES)
    z = _dot(a1.astype(BF16), wa2_ref[...]) + ba_ref[...]
    gk_ref[...] = (jnp.minimum(z, 0.0) - jnp.log1p(jnp.exp(-jnp.abs(z)))) * (1.0 / GATE_NORM)


def _pre_mixer(x, w, rope_tabs, tm, rope_period_blocks):
    n, d = x.shape
    grid = (n // tm,)

    def tok(width):
        return pl.BlockSpec((tm, width), lambda i: (i, 0))

    def full(a):
        return pl.BlockSpec(a.shape, lambda i: (0,) * a.ndim)

    rope_spec = pl.BlockSpec((tm, LANES), lambda i: (i % rope_period_blocks, 0))
    weights = (w["g_ffn1"], w["w_ffn1_gu"], w["w_ffn1_down"], w["g_mix"], w["w_in"],
               w["w_gla_a2"], w["b_gla_a"])
    out_widths = (d, ATT_W, ATT_W, ATT_W, GK_W, GK_W, GV_W, GK_W, GV_W)
    return pl.pallas_call(
        _pre_mixer_kernel,
        grid=grid,
        in_specs=[tok(d)] + [full(a) for a in weights] + [rope_spec] * 3,
        out_specs=[tok(wd) for wd in out_widths],
        out_shape=[jax.ShapeDtypeStruct((n, wd), F32) for wd in out_widths],
        compiler_params=pltpu.CompilerParams(
            dimension_semantics=("parallel",), vmem_limit_bytes=VMEM_LIMIT_BYTES),
        name="pre_mixer",
    )(x, *weights, *rope_tabs)


def _post_mixer_kernel(h1_ref, att_ref, gla_ref, p_ref, wout_ref, g2_ref, wgu_ref, wdn_ref,
                       gp_ref, wpg_ref, wpp_ref, gf_ref, y_ref):
    mix = _dot(att_ref[...], wout_ref[:ATT_W, :]) + _dot(gla_ref[...], wout_ref[ATT_W:, :])
    h2 = h1_ref[...] + mix
    h3 = _swiglu_half_step(h2, g2_ref, wgu_ref, wdn_ref)
    gate = jax.nn.sigmoid(_dot(_rms(h3, gp_ref[...]).astype(BF16), wpg_ref[...]))
    h4 = h3 + gate * _dot(p_ref[...].astype(BF16), wpp_ref[...])
    y_ref[...] = _rms(h4, gf_ref[...])


def _post_mixer(h1, att, gla, p, w, tm):
    n, d = h1.shape
    grid = (n // tm,)

    def tok(width):
        return pl.BlockSpec((tm, width), lambda i: (i, 0))

    def full(a):
        return pl.BlockSpec(a.shape, lambda i: (0,) * a.ndim)

    weights = (w["w_out"], w["g_ffn2"], w["w_ffn2_gu"], w["w_ffn2_down"], w["g_ple"],
               w["w_ple_gate"], w["w_ple_proj"], w["g_final"])
    return pl.pallas_call(
        _post_mixer_kernel,
        grid=grid,
        in_specs=[tok(d), tok(ATT_W), tok(GV_W), tok(p.shape[1])] + [full(a) for a in weights],
        out_specs=tok(d),
        out_shape=jax.ShapeDtypeStruct((n, d), F32),
        compiler_params=pltpu.CompilerParams(
            dimension_semantics=("parallel",), vmem_limit_bytes=VMEM_LIMIT_BYTES),
        name="post_mixer",
    )(h1, att, gla, p, *weights)


def _attn_prompt_kernel(q_ref, k_ref, v_ref, o_ref, oacc_ref, lse_ref):
    seq = q_ref.shape[1]
    qv, kv, vv = q_ref.at[0], k_ref.at[0], v_ref.at[0]
    blk = ATT_BLK
    lane = lax.broadcasted_iota(jnp.int32, (1, LANES), 1)
    head0 = lane < HD_A
    qi = lax.broadcasted_iota(jnp.int32, (blk, 2 * blk), 0)
    kj = lax.broadcasted_iota(jnp.int32, (blk, 2 * blk), 1)
    in_cur = (kj >= blk) & (kj - blk <= qi)
    in_prev = (kj < blk) & (qi <= kj)
    bias_first = jnp.where(in_cur, 0.0, NEG)
    bias_rest = jnp.where(in_cur | in_prev, 0.0, NEG)

    for bi, (window, dil) in enumerate(DILATED_BRANCHES):
        assert window // dil == blk
        nb = seq // (dil * blk)

        def unit(idx, dil=dil, nb=nb, bi=bi):
            r = idx // nb
            n = idx % nb
            start = r + n * (blk * dil)
            pstart = jnp.maximum(start - blk * dil, r)
            if dil == 1:
                start = pl.multiple_of(start, blk)
                pstart = pl.multiple_of(pstart, blk)

            def rows(ref, s0):
                if dil == 1:
                    return ref[pl.ds(s0, blk), :]
                return ref[pl.ds(s0, blk, stride=dil), :]

            qb = rows(qv, start)
            kcat = jnp.concatenate([rows(kv, pstart), rows(kv, start)], axis=0).astype(BF16)
            vcat = jnp.concatenate([rows(vv, pstart), rows(vv, start)], axis=0).astype(BF16)
            bias = jnp.where(n > 0, bias_rest, bias_first)
            outs, lses = [], []
            for hmask in (head0, ~head0):
                qh = jnp.where(hmask, qb, 0.0).astype(BF16)
                s = _dot_nt(qh, kcat) + bias
                m = jnp.max(s, axis=-1, keepdims=True)
                p = jnp.exp(s - m)
                den = jnp.sum(p, axis=-1, keepdims=True)
                outs.append(_dot(p.astype(BF16), vcat) / den)
                lses.append(m + jnp.log(den))
            o = jnp.where(head0, outs[0], outs[1])
            l = jnp.where(head0, lses[0], lses[1])
            if dil == 1:
                oacc_ref[bi, pl.ds(start, blk), :] = o
                lse_ref[bi, pl.ds(start, blk), :] = l
            else:
                oacc_ref.at[bi][pl.ds(start, blk, stride=dil), :] = o
                lse_ref.at[bi][pl.ds(start, blk, stride=dil), :] = l

        def body(it, carry, unit=unit):
            for u in range(ATT_UNROLL):
                unit(it * ATT_UNROLL + u)
            return carry

        lax.fori_loop(0, seq // (blk * ATT_UNROLL), body, 0)

    rows_per = 256
    for c in range(seq // rows_per):
        sl = slice(c * rows_per, (c + 1) * rows_per)
        l0, l1, l2 = lse_ref[0, sl, :], lse_ref[1, sl, :], lse_ref[2, sl, :]
        mx = jnp.maximum(jnp.maximum(l0, l1), l2)
        w0, w1, w2 = jnp.exp(l0 - mx), jnp.exp(l1 - mx), jnp.exp(l2 - mx)
        tot = w0 + w1 + w2
        num = w0 * oacc_ref[0, sl, :] + w1 * oacc_ref[1, sl, :] + w2 * oacc_ref[2, sl, :]
        o_ref[0, sl, :] = (num / tot).astype(o_ref.dtype)


def _attn_prompt(q, k, v):
    b, seq, width = q.shape
    spec = pl.BlockSpec((1, seq, LANES), lambda i, j: (i, 0, j))
    nbr = len(DILATED_BRANCHES)
    return pl.pallas_call(
        _attn_prompt_kernel,
        grid=(b, width // LANES),
        in_specs=[spec] * 3,
        out_specs=spec,
        out_shape=jax.ShapeDtypeStruct((b, seq, width), BF16),
        scratch_shapes=[pltpu.VMEM((nbr, seq, LANES), F32), pltpu.VMEM((nbr, seq, LANES), F32)],
        compiler_params=pltpu.CompilerParams(
            dimension_semantics=("parallel", "parallel"), vmem_limit_bytes=VMEM_LIMIT_BYTES),
        name="attn_prompt",
    )(q, k, v)


T_PAD = 8


def _attn_sample_kernel(q_ref, knp_ref, vnp_ref, kt_ref, vt_ref, att_ref, ko_ref, vo_ref, *, t_new):
    n_heads, w = kt_ref.shape[1], kt_ref.shape[3]
    tq = lax.broadcasted_iota(jnp.int32, (T_PAD, w), 0)
    tok = lax.broadcasted_iota(jnp.int32, (T_PAD, w), 1)
    tqn = lax.broadcasted_iota(jnp.int32, (T_PAD, LANES), 0)
    un = lax.broadcasted_iota(jnp.int32, (T_PAD, LANES), 1) - (LANES - t_new)
    lane_new = lax.broadcasted_iota(jnp.int32, (1, LANES), 1) >= LANES - t_new
    cache_masks, new_masks = [], []
    for window, dil in DILATED_BRANCHES:
        assert w % dil == 0 and dil & (dil - 1) == 0
        cache_masks.append((tq < t_new) & (tok >= w + tq - window)
                           & (((tok - tq) & (dil - 1)) == 0))
        new_masks.append((tqn < t_new) & (un >= 0) & (un <= tqn)
                         & (((tqn - un) & (dil - 1)) == 0))

    for h in range(n_heads):
        q = q_ref[0, h].astype(BF16)
        kt, vt = kt_ref[0, h], vt_ref[0, h]
        knp, vnp = knp_ref[0, h], vnp_ref[0, h]
        sc = _dot(q, kt.astype(BF16))
        sn = _dot(q, knp.astype(BF16))
        pcs, pns, dens, lses = [], [], [], []
        for cm, nm in zip(cache_masks, new_masks):
            scm = jnp.where(cm, sc, NEG)
            snm = jnp.where(nm, sn, NEG)
            m = jnp.maximum(jnp.max(scm, axis=-1, keepdims=True), jnp.max(snm, axis=-1, keepdims=True))
            pc, pn = jnp.exp(scm - m), jnp.exp(snm - m)
            den = jnp.sum(pc, axis=-1, keepdims=True) + jnp.sum(pn, axis=-1, keepdims=True)
            pcs.append(pc)
            pns.append(pn)
            dens.append(den)
            lses.append(m + jnp.log(den))
        o_all = (_dot_nt(jnp.concatenate(pcs, axis=0).astype(BF16), vt.astype(BF16))
                 + _dot_nt(jnp.concatenate(pns, axis=0).astype(BF16), vnp.astype(BF16)))
        mx = jnp.maximum(jnp.maximum(lses[0], lses[1]), lses[2])
        num, tot = 0.0, 0.0
        for bi in range(len(DILATED_BRANCHES)):
            wgt = jnp.exp(lses[bi] - mx)
            num = num + wgt * (o_all[bi * T_PAD:(bi + 1) * T_PAD] / dens[bi])
            tot = tot + wgt
        att_ref[0, h] = num / tot
        for src, newp, dst in ((kt, knp, ko_ref), (vt, vnp, vo_ref)):
            rolled = pltpu.roll(src, w - t_new, 1)
            dst[0, h, :, :w - LANES] = rolled[:, :w - LANES]
            dst[0, h, :, w - LANES:] = jnp.where(lane_new, newp, rolled[:, w - LANES:])


def _attn_sample(q4, knp, vnp, kt, vt, t_new):
    b, h, _, dh = q4.shape
    w = kt.shape[3]
    q_spec = pl.BlockSpec((1, h, T_PAD, dh), lambda i: (i, 0, 0, 0))
    new_spec = pl.BlockSpec((1, h, dh, LANES), lambda i: (i, 0, 0, 0))
    cache_spec = pl.BlockSpec((1, h, dh, w), lambda i: (i, 0, 0, 0))
    return pl.pallas_call(
        functools.partial(_attn_sample_kernel, t_new=t_new),
        grid=(b,),
        in_specs=[q_spec, new_spec, new_spec, cache_spec, cache_spec],
        out_specs=[q_spec, cache_spec, cache_spec],
        out_shape=[jax.ShapeDtypeStruct(q4.shape, F32), jax.ShapeDtypeStruct(kt.shape, F32),
                   jax.ShapeDtypeStruct(vt.shape, F32)],
        compiler_params=pltpu.CompilerParams(
            dimension_semantics=("parallel",), vmem_limit_bytes=VMEM_LIMIT_BYTES),
        name="attn_sample",
    )(q4, knp, vnp, kt, vt)


def _gla_decay_matrix(chunk):
    t = np.arange(chunk)[:, None]
    u = np.arange(chunk)[None, :]
    blocks = [u <= t, u > t]
    s = chunk // 2
    while s >= 1:
        mid = (t // (2 * s)) * (2 * s) + s
        upper = (t >= mid) & (u >= mid) & (u <= t)
        lower = (t < mid) & (u > t) & (u < mid)
        blocks.append(upper | lower)
        s //= 2
    return np.concatenate(blocks, axis=0).astype(np.float32)


def _gla_kernel(q_ref, k_ref, v_ref, g_ref, gb_ref, s0_ref, m_ref, gout_ref,
                o_ref, sfin_ref, st_ref, *, chunk):
    j = pl.program_id(1)
    n_lvl = int(math.log2(chunk))
    tb = q_ref.shape[1]
    hk = lax.broadcasted_iota(jnp.int32, (1, GK_W), 1) // DK_B
    hv = lax.broadcasted_iota(jnp.int32, (1, GV_W), 1) // DV_B
    hv_rows = lax.broadcasted_iota(jnp.int32, (GV_W, 1), 0) // DV_B
    state_mask = hv_rows == hk

    @pl.when(j == 0)
    def _():
        s0t = s0_ref[0].reshape(GK_W, DV_B).T
        st_ref[...] = jnp.where(state_mask, jnp.concatenate([s0t] * H_B, axis=0), 0.0)

    pi = lax.broadcasted_iota(jnp.int32, (chunk, H_B * chunk), 0)
    pj = lax.broadcasted_iota(jnp.int32, (chunk, H_B * chunk), 1) % chunk
    pair_masks = []
    s = chunk // 2
    while s >= 1:
        same = (pi // (2 * s)) == (pj // (2 * s))
        pair_masks.append(same & (pi % (2 * s) >= s) & (pj % (2 * s) < s))
        s //= 2
    pair_masks.append(pi == pj)
    a_head = lax.broadcasted_iota(jnp.int32, (H_B * chunk, 1), 0) // chunk

    mdec = m_ref[...]
    gout = gout_ref[...]
    for c in range(tb // chunk):
        sl = slice(c * chunk, (c + 1) * chunk)
        q, k, v, g = q_ref[0, sl, :], k_ref[0, sl, :], v_ref[0, sl, :], g_ref[0, sl, :]
        g_hi = g.astype(BF16)
        r1 = g - g_hi.astype(F32)
        g_mid = r1.astype(BF16)
        g_lo = (r1 - g_mid.astype(F32)).astype(BF16)
        e = _dot(mdec, g_hi) + _dot(mdec, g_mid) + _dot(mdec, g_lo)
        x = jnp.exp(e)
        x_cum = x[0:chunk]
        x_rem = x[chunk:2 * chunk]
        a_all = jnp.zeros((chunk, H_B * chunk), F32)
        for lvl in range(n_lvl + 1):
            if lvl < n_lvl:
                xl = x[(2 + lvl) * chunk:(3 + lvl) * chunk]
                ql, kl = q * xl, k * xl
            else:
                ql, kl = q, k
            kl_bd = jnp.where(a_head == hk, jnp.concatenate([kl] * H_B, axis=0), 0.0)
            r = _dot_nt(ql.astype(BF16), kl_bd.astype(BF16))
            a_all = jnp.where(pair_masks[lvl], r, a_all)
        v_bd = jnp.where(a_head == hv, jnp.concatenate([v] * H_B, axis=0), 0.0)
        st = st_ref[...]
        o = (_dot_nt((q * x_cum).astype(BF16), st.astype(BF16))
             + _dot(a_all.astype(BF16), v_bd.astype(BF16)))
        upd = _dot_tn(v.astype(BF16), (k * x_rem).astype(BF16))
        st_ref[...] = st * x_cum[chunk - 1:chunk, :] + jnp.where(state_mask, upd, 0.0)
        gb = gb_ref[0, sl, :]
        for h in range(H_B):
            hs = slice(h * DV_B, (h + 1) * DV_B)
            gh = gb[:, hs]
            o_ref[0, sl, hs] = (_rms(o[:, hs], gout) * (gh * jax.nn.sigmoid(gh))).astype(o_ref.dtype)

    @pl.when(j == pl.num_programs(1) - 1)
    def _():
        st = st_ref[...]
        s_t = st[0:DV_B]
        for h in range(1, H_B):
            s_t = s_t + st[h * DV_B:(h + 1) * DV_B]
        sfin_ref[0] = s_t.T.reshape(H_B, DK_B, DV_B)


def _gla(q, k, v, g, gb, s0, g_out, chunk, tb):
    b, t, _ = q.shape
    mdec = jnp.asarray(_gla_decay_matrix(chunk), dtype=BF16)

    def tok(width):
        return pl.BlockSpec((1, tb, width), lambda i, j: (i, j, 0))

    state_spec = pl.BlockSpec((1, H_B, DK_B, DV_B), lambda i, j: (i, 0, 0, 0))
    return pl.pallas_call(
        functools.partial(_gla_kernel, chunk=chunk),
        grid=(b, t // tb),
        in_specs=[tok(GK_W), tok(GK_W), tok(GV_W), tok(GK_W), tok(GV_W), state_spec,
                  pl.BlockSpec(mdec.shape, lambda i, j: (0, 0)),
                  pl.BlockSpec(g_out.shape, lambda i, j: (0, 0))],
        out_specs=[tok(GV_W), state_spec],
        out_shape=[jax.ShapeDtypeStruct((b, t, GV_W), BF16),
                   jax.ShapeDtypeStruct((b, H_B, DK_B, DV_B), F32)],
        scratch_shapes=[pltpu.VMEM((GV_W, GK_W), F32)],
        compiler_params=pltpu.CompilerParams(
            dimension_semantics=("parallel", "arbitrary"), vmem_limit_bytes=VMEM_LIMIT_BYTES),
        name="gla",
    )(q, k, v, g, gb, s0, mdec, g_out)


def _rope_tables(pos):
    half = ROT_DIM // 2
    inv_freq = ROPE_THETA ** (-jnp.arange(half, dtype=F32) * (2.0 / ROT_DIM))
    ang = pos.astype(F32)[:, None] * inv_freq[None, :]
    cos, sin = jnp.cos(ang), jnp.sin(ang)
    n = pos.shape[0]
    ones = jnp.ones((n, HD_A - ROT_DIM), F32)
    zeros = jnp.zeros((n, HD_A - ROT_DIM), F32)
    zh = jnp.zeros((n, half), F32)
    c = jnp.concatenate([cos, cos, ones], axis=1)
    sa = jnp.concatenate([zh, sin, zeros], axis=1)
    sb = jnp.concatenate([-sin, zh, zeros], axis=1)
    rep = LANES // HD_A
    return tuple(jnp.tile(tab, (1, rep)) for tab in (c, sa, sb))


def _prep_weights(g_ffn1, w_ffn1_gu, w_ffn1_down, g_mix, w_in, w_gla_a2, b_gla_a, g_gla_out, w_out,
                  g_ffn2, w_ffn2_gu, w_ffn2_down, g_ple, w_ple_gate, w_ple_proj, g_final):
    a1_lo = 3 * ATT_W + 2 * GK_W + GV_W
    win = w_in[0]
    win = jnp.concatenate(
        [win[:, :a1_lo], win[:, a1_lo + GATE_RANK:], win[:, a1_lo:a1_lo + GATE_RANK],
         jnp.zeros((win.shape[0], LANES - GATE_RANK), win.dtype)], axis=1)
    wa2 = jnp.concatenate(
        [w_gla_a2[0], jnp.zeros((LANES - GATE_RANK, GK_W), w_gla_a2.dtype)], axis=0)
    return {
        "g_ffn1": g_ffn1, "w_ffn1_gu": w_ffn1_gu[0].astype(BF16),
        "w_ffn1_down": w_ffn1_down[0].astype(BF16), "g_mix": g_mix,
        "w_in": win.astype(BF16), "w_gla_a2": wa2.astype(BF16), "b_gla_a": b_gla_a,
        "g_gla_out": g_gla_out, "w_out": w_out[0].astype(BF16), "g_ffn2": g_ffn2,
        "w_ffn2_gu": w_ffn2_gu[0].astype(BF16), "w_ffn2_down": w_ffn2_down[0].astype(BF16),
        "g_ple": g_ple, "w_ple_gate": w_ple_gate[0].astype(BF16),
        "w_ple_proj": w_ple_proj[0].astype(BF16), "g_final": g_final.reshape(1, -1),
    }


def kernel(x_prompt, x_sample, cache_k_win, cache_v_win, state_gla, p_prompt, p_sample, g_ffn1, w_ffn1_gu, w_ffn1_down, g_mix, w_in, w_gla_a2, b_gla_a, g_gla_out, w_out, g_ffn2, w_ffn2_gu, w_ffn2_down, g_ple, w_ple_gate, w_ple_proj, g_final):
    assert w_in.shape[0] == 1, "single-layer step"
    w = _prep_weights(g_ffn1, w_ffn1_gu, w_ffn1_down, g_mix, w_in, w_gla_a2, b_gla_a, g_gla_out,
                      w_out, g_ffn2, w_ffn2_gu, w_ffn2_down, g_ple, w_ple_gate, w_ple_proj, g_final)
    bp, seq, d = x_prompt.shape
    bs, t_new, _ = x_sample.shape
    win_buf = cache_k_win.shape[2]
    assert seq == DILATED_BRANCHES[-1][0] == win_buf

    tm = 512
    rope_p = _rope_tables(jnp.arange(seq, dtype=jnp.int32))
    (h1, qa, ka, va, qb, kb, vb, gk, gb) = _pre_mixer(
        x_prompt.reshape(bp * seq, d), w, rope_p, tm, seq // tm)
    r3 = lambda a: a.reshape(bp, seq, a.shape[-1])
    att_p = _attn_prompt(r3(qa), r3(ka), r3(va))
    gla_p, state_p = _gla(r3(qb), r3(kb), r3(vb), r3(gk), r3(gb),
                          jnp.zeros((bp, H_B, DK_B, DV_B), F32), w["g_gla_out"],
                          chunk=GLA_CHUNK, tb=4 * GLA_CHUNK)
    y_prompt = _post_mixer(h1, att_p.reshape(bp * seq, ATT_W), gla_p.reshape(bp * seq, GV_W),
                           p_prompt[0].reshape(bp * seq, -1), w, tm).reshape(bp, seq, d)
    k_win_prompt = ka.reshape(1, bp, seq, H_A, HD_A)
    v_win_prompt = va.reshape(1, bp, seq, H_A, HD_A)

    ns = bs * t_new
    rope_s = _rope_tables(jnp.tile(PAST_LEN + jnp.arange(t_new, dtype=jnp.int32), bs))
    (h1s, qas, kas, vas, qbs, kbs, vbs, gks, gbs) = _pre_mixer(
        x_sample.reshape(ns, d), w, rope_s, ns, 1)
    r5 = lambda a: a.reshape(bs, t_new, H_A, HD_A)
    q4 = jnp.pad(r5(qas).transpose(0, 2, 1, 3), ((0, 0), (0, 0), (0, T_PAD - t_new), (0, 0)))
    new_cols = lambda a: jnp.pad(r5(a).transpose(0, 2, 3, 1),
                                 ((0, 0), (0, 0), (0, 0), (LANES - t_new, 0)))
    tok_last = lambda c: c[0].transpose(0, 2, 3, 1)
    att4, kto, vto = _attn_sample(q4, new_cols(kas), new_cols(vas), tok_last(cache_k_win),
                                  tok_last(cache_v_win), t_new)
    att_s = att4[:, :, :t_new].transpose(0, 2, 1, 3)
    k_win_sample = kto.transpose(0, 3, 1, 2)
    v_win_sample = vto.transpose(0, 3, 1, 2)
    s_chunk = 16
    pad = lambda a: jnp.pad(a.reshape(bs, t_new, a.shape[-1]), ((0, 0), (0, s_chunk - t_new), (0, 0)))
    gla_s, state_s = _gla(pad(qbs), pad(kbs), pad(vbs), pad(gks), pad(gbs), state_gla[0],
                          w["g_gla_out"], chunk=s_chunk, tb=s_chunk)
    y_sample = _post_mixer(h1s, att_s.reshape(ns, ATT_W).astype(BF16),
                           gla_s[:, :t_new].reshape(ns, GV_W), p_sample[0].reshape(ns, -1),
                           w, ns).reshape(bs, t_new, d)

    return (y_prompt, y_sample, k_win_prompt, v_win_prompt, state_p[None],
            k_win_sample[None], v_win_sample[None], state_s[None])
```

```python
import functools
import math

import jax
import jax.numpy as jnp
import numpy as np
from jax import lax
from jax.experimental import pallas as pl
from jax.experimental.pallas import tpu as pltpu

F32 = jnp.float32
BF16 = jnp.bfloat16

H_A = 8
HD_A = 64
ATT_W = H_A * HD_A
DILATED_BRANCHES = ((128, 1), (512, 4), (2048, 16))
ROT_DIM = HD_A // 4
ROPE_THETA = 500000.0
PAST_LEN = 8192
H_B = 4
DK_B = 64
DV_B = 128
GK_W = H_B * DK_B
GV_W = H_B * DV_B
GATE_RANK = 16
GATE_NORM = 16.0
GLA_CHUNK = 64
EPS = 1e-6
NEG = -1e30

LANES = 128
VMEM_LIMIT_BYTES = 56 * 1024 * 1024
ATT_BLK = 128
ATT_UNROLL = 8


def _rms(x, g):
    return x * lax.rsqrt(jnp.mean(x * x, axis=-1, keepdims=True) + EPS) * g


def _dot(a, b):
    return jnp.dot(a, b, preferred_element_type=F32)


def _dot_nt(a, b):
    return lax.dot_general(a, b, (((1,), (1,)), ((), ())), preferred_element_type=F32)


def _dot_tn(a, b):
    return lax.dot_general(a, b, (((0,), (0,)), ((), ())), preferred_element_type=F32)


def _swiglu_half_step(x, g_ref, wgu_ref, wdn_ref):
    d_ff = wdn_ref.shape[0]
    xn = _rms(x, g_ref[...]).astype(BF16)
    n_chunk = 2 if d_ff % (2 * LANES) == 0 else 1
    fc = d_ff // n_chunk
    acc = None
    for c in range(n_chunk):
        gate = _dot(xn, wgu_ref[:, c * fc:(c + 1) * fc])
        up = _dot(xn, wgu_ref[:, d_ff + c * fc:d_ff + (c + 1) * fc])
        act = (gate * jax.nn.sigmoid(gate) * up).astype(BF16)
        part = _dot(act, wdn_ref[c * fc:(c + 1) * fc, :])
        acc = part if acc is None else acc + part
    return x + 0.5 * acc


def _pre_mixer_kernel(x_ref, g1_ref, wgu_ref, wdn_ref, gm_ref, win_ref, wa2_ref, ba_ref,
                      rc_ref, rsa_ref, rsb_ref,
                      h1_ref, qa_ref, ka_ref, va_ref, qb_ref, kb_ref, vb_ref, gk_ref, gb_ref,
                      kat_ref, vat_ref):
    h1 = _swiglu_half_step(x_ref[...], g1_ref, wgu_ref, wdn_ref)
    h1_ref[...] = h1
    xm = _rms(h1, gm_ref[...]).astype(BF16)

    def proj(lo, width):
        return _dot(xm, win_ref[:, lo:lo + width])

    rc, rsa, rsb = rc_ref[...], rsa_ref[...], rsb_ref[...]

    def rope(t, scale):
        cols = []
        for c in range(ATT_W // LANES):
            blk = t[:, c * LANES:(c + 1) * LANES]
            rot = (blk * rc + pltpu.roll(blk, ROT_DIM // 2, 1) * rsa
                   + pltpu.roll(blk, LANES - ROT_DIM // 2, 1) * rsb)
            cols.append(rot * scale if scale != 1.0 else rot)
        return jnp.concatenate(cols, axis=1)

    qa_ref[...] = rope(proj(0, ATT_W), HD_A ** -0.5)
    ka = rope(proj(ATT_W, ATT_W), 1.0)
    va = proj(2 * ATT_W, ATT_W)
    ka_ref[...] = ka
    va_ref[...] = va
    kat_ref[0] = ka.T
    vat_ref[0] = va.T
    off = 3 * ATT_W
    qb_ref[...] = proj(off, GK_W) * (DK_B ** -0.5)
    kb_ref[...] = proj(off + GK_W, GK_W)
    vb_ref[...] = proj(off + 2 * GK_W, GV_W)
    gb_ref[...] = proj(off + 2 * GK_W + GV_W, GV_W)
    a1 = proj(off + 2 * GK_W + 2 * GV_W, LANES)
    z = _dot(a1.astype(BF16), wa2_ref[...]) + ba_ref[...]
    gk_ref[...] = (jnp.minimum(z, 0.0) - jnp.log1p(jnp.exp(-jnp.abs(z)))) * (1.0 / GATE_NORM)


def _pre_mixer(x, w, rope_tabs, tm, rope_period_blocks):
    n, d = x.shape
    grid = (n // tm,)

    def tok(width):
        return pl.BlockSpec((tm, width), lambda i: (i, 0))

    def full(a):
        return pl.BlockSpec(a.shape, lambda i: (0,) * a.ndim)

    rope_spec = pl.BlockSpec((tm, LANES), lambda i: (i % rope_period_blocks, 0))
    weights = (w["g_ffn1"], w["w_ffn1_gu"], w["w_ffn1_down"], w["g_mix"], w["w_in"],
               w["w_gla_a2"], w["b_gla_a"])
    out_widths = (d, ATT_W, ATT_W, ATT_W, GK_W, GK_W, GV_W, GK_W, GV_W)
    tps = tm * rope_period_blocks
    tok_last = pl.BlockSpec((1, ATT_W, tm),
                            lambda i: (i // rope_period_blocks, 0, i % rope_period_blocks))
    tok_last_shape = jax.ShapeDtypeStruct((n // tps, ATT_W, tps), F32)
    return pl.pallas_call(
        _pre_mixer_kernel,
        grid=grid,
        in_specs=[tok(d)] + [full(a) for a in weights] + [rope_spec] * 3,
        out_specs=[tok(wd) for wd in out_widths] + [tok_last] * 2,
        out_shape=[jax.ShapeDtypeStruct((n, wd), F32) for wd in out_widths] + [tok_last_shape] * 2,
        compiler_params=pltpu.CompilerParams(
            dimension_semantics=("parallel",), vmem_limit_bytes=VMEM_LIMIT_BYTES),
        name="pre_mixer",
    )(x, *weights, *rope_tabs)


def _post_mixer_kernel(h1_ref, att_ref, gla_ref, p_ref, wout_ref, g2_ref, wgu_ref, wdn_ref,
                       gp_ref, wpg_ref, wpp_ref, gf_ref, y_ref):
    mix = _dot(att_ref[...], wout_ref[:ATT_W, :]) + _dot(gla_ref[...], wout_ref[ATT_W:, :])
    h2 = h1_ref[...] + mix
    h3 = _swiglu_half_step(h2, g2_ref, wgu_ref, wdn_ref)
    gate = jax.nn.sigmoid(_dot(_rms(h3, gp_ref[...]).astype(BF16), wpg_ref[...]))
    h4 = h3 + gate * _dot(p_ref[...].astype(BF16), wpp_ref[...])
    y_ref[...] = _rms(h4, gf_ref[...])


def _post_mixer(h1, att, gla, p, w, tm):
    n, d = h1.shape
    grid = (n // tm,)

    def tok(width):
        return pl.BlockSpec((tm, width), lambda i: (i, 0))

    def full(a):
        return pl.BlockSpec(a.shape, lambda i: (0,) * a.ndim)

    weights = (w["w_out"], w["g_ffn2"], w["w_ffn2_gu"], w["w_ffn2_down"], w["g_ple"],
               w["w_ple_gate"], w["w_ple_proj"], w["g_final"])
    return pl.pallas_call(
        _post_mixer_kernel,
        grid=grid,
        in_specs=[tok(d), tok(ATT_W), tok(GV_W), tok(p.shape[1])] + [full(a) for a in weights],
        out_specs=tok(d),
        out_shape=jax.ShapeDtypeStruct((n, d), F32),
        compiler_params=pltpu.CompilerParams(
            dimension_semantics=("parallel",), vmem_limit_bytes=VMEM_LIMIT_BYTES),
        name="post_mixer",
    )(h1, att, gla, p, *weights)


def _attn_prompt_kernel(q_ref, k_ref, v_ref, o_ref, oacc_ref, lse_ref):
    seq = q_ref.shape[1]
    qv, kv, vv = q_ref.at[0], k_ref.at[0], v_ref.at[0]
    blk = ATT_BLK
    lane = lax.broadcasted_iota(jnp.int32, (1, LANES), 1)
    head0 = lane < HD_A
    qi = lax.broadcasted_iota(jnp.int32, (blk, 2 * blk), 0)
    kj = lax.broadcasted_iota(jnp.int32, (blk, 2 * blk), 1)
    in_cur = (kj >= blk) & (kj - blk <= qi)
    in_prev = (kj < blk) & (qi <= kj)
    bias_first = jnp.where(in_cur, 0.0, NEG)
    bias_rest = jnp.where(in_cur | in_prev, 0.0, NEG)

    for bi, (window, dil) in enumerate(DILATED_BRANCHES):
        assert window // dil == blk
        nb = seq // (dil * blk)

        def unit(idx, dil=dil, nb=nb, bi=bi):
            r = idx // nb
            n = idx % nb
            start = r + n * (blk * dil)
            pstart = jnp.maximum(start - blk * dil, r)
            if dil == 1:
                start = pl.multiple_of(start, blk)
                pstart = pl.multiple_of(pstart, blk)

            def rows(ref, s0):
                if dil == 1:
                    return ref[pl.ds(s0, blk), :]
                return ref[pl.ds(s0, blk, stride=dil), :]

            qb = rows(qv, start)
            kcat = jnp.concatenate([rows(kv, pstart), rows(kv, start)], axis=0).astype(BF16)
            vcat = jnp.concatenate([rows(vv, pstart), rows(vv, start)], axis=0).astype(BF16)
            bias = jnp.where(n > 0, bias_rest, bias_first)
            outs, lses = [], []
            for hmask in (head0, ~head0):
                qh = jnp.where(hmask, qb, 0.0).astype(BF16)
                s = _dot_nt(qh, kcat) + bias
                m = jnp.max(s, axis=-1, keepdims=True)
                p = jnp.exp(s - m)
                den = jnp.sum(p, axis=-1, keepdims=True)
                outs.append(_dot(p.astype(BF16), vcat) / den)
                lses.append(m + jnp.log(den))
            o = jnp.where(head0, outs[0], outs[1])
            l = jnp.where(head0, lses[0], lses[1])
            if dil == 1:
                oacc_ref[bi, pl.ds(start, blk), :] = o
                lse_ref[bi, pl.ds(start, blk), :] = l
            else:
                oacc_ref.at[bi][pl.ds(start, blk, stride=dil), :] = o
                lse_ref.at[bi][pl.ds(start, blk, stride=dil), :] = l

        def body(it, carry, unit=unit):
            for u in range(ATT_UNROLL):
                unit(it * ATT_UNROLL + u)
            return carry

        lax.fori_loop(0, seq // (blk * ATT_UNROLL), body, 0)

    rows_per = 256
    for c in range(seq // rows_per):
        sl = slice(c * rows_per, (c + 1) * rows_per)
        l0, l1, l2 = lse_ref[0, sl, :], lse_ref[1, sl, :], lse_ref[2, sl, :]
        mx = jnp.maximum(jnp.maximum(l0, l1), l2)
        w0, w1, w2 = jnp.exp(l0 - mx), jnp.exp(l1 - mx), jnp.exp(l2 - mx)
        tot = w0 + w1 + w2
        num = w0 * oacc_ref[0, sl, :] + w1 * oacc_ref[1, sl, :] + w2 * oacc_ref[2, sl, :]
        o_ref[0, sl, :] = (num / tot).astype(o_ref.dtype)


def _attn_prompt(q, k, v):
    b, seq, width = q.shape
    spec = pl.BlockSpec((1, seq, LANES), lambda i, j: (i, 0, j))
    nbr = len(DILATED_BRANCHES)
    return pl.pallas_call(
        _attn_prompt_kernel,
        grid=(b, width // LANES),
        in_specs=[spec] * 3,
        out_specs=spec,
        out_shape=jax.ShapeDtypeStruct((b, seq, width), BF16),
        scratch_shapes=[pltpu.VMEM((nbr, seq, LANES), F32), pltpu.VMEM((nbr, seq, LANES), F32)],
        compiler_params=pltpu.CompilerParams(
            dimension_semantics=("parallel", "parallel"), vmem_limit_bytes=VMEM_LIMIT_BYTES),
        name="attn_prompt",
    )(q, k, v)


T_PAD = 8


def _attn_sample_kernel(q_ref, knp_ref, vnp_ref, kt_ref, vt_ref, att_ref, ko_ref, vo_ref, *, t_new):
    n_heads, w = kt_ref.shape[1], kt_ref.shape[3]
    tq = lax.broadcasted_iota(jnp.int32, (T_PAD, w), 0)
    tok = lax.broadcasted_iota(jnp.int32, (T_PAD, w), 1)
    tqn = lax.broadcasted_iota(jnp.int32, (T_PAD, LANES), 0)
    un = lax.broadcasted_iota(jnp.int32, (T_PAD, LANES), 1) - (LANES - t_new)
    lane_new = lax.broadcasted_iota(jnp.int32, (1, LANES), 1) >= LANES - t_new
    cache_masks, new_masks = [], []
    for window, dil in DILATED_BRANCHES:
        assert w % dil == 0 and dil & (dil - 1) == 0
        cache_masks.append((tq < t_new) & (tok >= w + tq - window)
                           & (((tok - tq) & (dil - 1)) == 0))
        new_masks.append((tqn < t_new) & (un >= 0) & (un <= tqn)
                         & (((tqn - un) & (dil - 1)) == 0))

    for h in range(n_heads):
        q = q_ref[0, h].astype(BF16)
        kt, vt = kt_ref[0, h], vt_ref[0, h]
        knp, vnp = knp_ref[0, h], vnp_ref[0, h]
        sc = _dot(q, kt.astype(BF16))
        sn = _dot(q, knp.astype(BF16))
        pcs, pns, dens, lses = [], [], [], []
        for cm, nm in zip(cache_masks, new_masks):
            scm = jnp.where(cm, sc, NEG)
            snm = jnp.where(nm, sn, NEG)
            m = jnp.maximum(jnp.max(scm, axis=-1, keepdims=True), jnp.max(snm, axis=-1, keepdims=True))
            pc, pn = jnp.exp(scm - m), jnp.exp(snm - m)
            den = jnp.sum(pc, axis=-1, keepdims=True) + jnp.sum(pn, axis=-1, keepdims=True)
            pcs.append(pc)
            pns.append(pn)
            dens.append(den)
            lses.append(m + jnp.log(den))
        o_all = (_dot_nt(jnp.concatenate(pcs, axis=0).astype(BF16), vt.astype(BF16))
                 + _dot_nt(jnp.concatenate(pns, axis=0).astype(BF16), vnp.astype(BF16)))
        mx = jnp.maximum(jnp.maximum(lses[0], lses[1]), lses[2])
        num, tot = 0.0, 0.0
        for bi in range(len(DILATED_BRANCHES)):
            wgt = jnp.exp(lses[bi] - mx)
            num = num + wgt * (o_all[bi * T_PAD:(bi + 1) * T_PAD] / dens[bi])
            tot = tot + wgt
        att_ref[0, h] = num / tot
        for src, newp, dst in ((kt, knp, ko_ref), (vt, vnp, vo_ref)):
            rolled = pltpu.roll(src, w - t_new, 1)
            dst[0, h, :, :w - LANES] = rolled[:, :w - LANES]
            dst[0, h, :, w - LANES:] = jnp.where(lane_new, newp, rolled[:, w - LANES:])


def _attn_sample(q4, knp, vnp, kt, vt, t_new):
    b, h, _, dh = q4.shape
    w = kt.shape[3]
    q_spec = pl.BlockSpec((1, h, T_PAD, dh), lambda i: (i, 0, 0, 0))
    new_spec = pl.BlockSpec((1, h, dh, LANES), lambda i: (i, 0, 0, 0))
    cache_spec = pl.BlockSpec((1, h, dh, w), lambda i: (i, 0, 0, 0))
    return pl.pallas_call(
        functools.partial(_attn_sample_kernel, t_new=t_new),
        grid=(b,),
        in_specs=[q_spec, new_spec, new_spec, cache_spec, cache_spec],
        out_specs=[q_spec, cache_spec, cache_spec],
        out_shape=[jax.ShapeDtypeStruct(q4.shape, F32), jax.ShapeDtypeStruct(kt.shape, F32),
                   jax.ShapeDtypeStruct(vt.shape, F32)],
        compiler_params=pltpu.CompilerParams(
            dimension_semantics=("parallel",), vmem_limit_bytes=VMEM_LIMIT_BYTES),
        name="attn_sample",
    )(q4, knp, vnp, kt, vt)


def _gla_decay_matrix(chunk):
    t = np.arange(chunk)[:, None]
    u = np.arange(chunk)[None, :]
    blocks = [u <= t, u > t]
    s = chunk // 2
    while s >= 1:
        mid = (t // (2 * s)) * (2 * s) + s
        upper = (t >= mid) & (u >= mid) & (u <= t)
        lower = (t < mid) & (u > t) & (u < mid)
        blocks.append(upper | lower)
        s //= 2
    return np.concatenate(blocks, axis=0).astype(np.float32)


def _gla_kernel(q_ref, k_ref, v_ref, g_ref, gb_ref, s0_ref, m_ref, gout_ref,
                o_ref, sfin_ref, st_ref, *, chunk):
    j = pl.program_id(1)
    n_lvl = int(math.log2(chunk))
    n_rows, tb = q_ref.shape[0], q_ref.shape[1]
    nc = tb // chunk
    hk = lax.broadcasted_iota(jnp.int32, (1, GK_W), 1) // DK_B
    hv = lax.broadcasted_iota(jnp.int32, (1, GV_W), 1) // DV_B
    hv_rows = lax.broadcasted_iota(jnp.int32, (GV_W, 1), 0) // DV_B
    state_mask = hv_rows == hk

    @pl.when(j == 0)
    def _():
        for r in range(n_rows):
            s0t = s0_ref[r].reshape(GK_W, DV_B).T
            st_ref[r] = jnp.where(state_mask, jnp.concatenate([s0t] * H_B, axis=0), 0.0)

    pi = lax.broadcasted_iota(jnp.int32, (chunk, H_B * chunk), 0)
    pj = lax.broadcasted_iota(jnp.int32, (chunk, H_B * chunk), 1) % chunk
    pair_masks = []
    s = chunk // 2
    while s >= 1:
        same = (pi // (2 * s)) == (pj // (2 * s))
        pair_masks.append(same & (pi % (2 * s) >= s) & (pj % (2 * s) < s))
        s //= 2
    pair_masks.append(pi == pj)
    a_head = lax.broadcasted_iota(jnp.int32, (H_B * chunk, 1), 0) // chunk
    bd_k = jnp.where(a_head == hk, 1.0, 0.0).astype(BF16)
    bd_v = jnp.where(a_head == hv, 1.0, 0.0).astype(BF16)

    mdec2 = m_ref[...]
    gout = gout_ref[...]
    units = [(r, slice(c * chunk, (c + 1) * chunk)) for r in range(n_rows) for c in range(nc)]
    qs = [q_ref[r, sl, :] for r, sl in units]
    ks = [k_ref[r, sl, :] for r, sl in units]
    xs = []
    for r, sl in units:
        g = g_ref[r, sl, :]
        g_hi = g.astype(BF16)
        g_lo = (g - g_hi.astype(F32)).astype(BF16)
        xs.append(jnp.exp(_dot(mdec2, jnp.concatenate([g_hi, g_lo], axis=0))))
    a_alls = [jnp.zeros((chunk, H_B * chunk), F32)] * len(units)
    for lvl in range(n_lvl + 1):
        for u in range(len(units)):
            if lvl < n_lvl:
                xl = xs[u][(2 + lvl) * chunk:(3 + lvl) * chunk]
                ql, kl = (qs[u] * xl).astype(BF16), (ks[u] * xl).astype(BF16)
            else:
                ql, kl = qs[u].astype(BF16), ks[u].astype(BF16)
            kl_bd = jnp.concatenate([kl] * H_B, axis=0) * bd_k
            a_alls[u] = jnp.where(pair_masks[lvl], _dot_nt(ql, kl_bd), a_alls[u])
    o_intra, q_dec, upds, decs = [], [], [], []
    for u, (r, sl) in enumerate(units):
        vb = v_ref[r, sl, :].astype(BF16)
        v_bd = jnp.concatenate([vb] * H_B, axis=0) * bd_v
        o_intra.append(_dot(a_alls[u].astype(BF16), v_bd))
        x_cum, x_rem = xs[u][0:chunk], xs[u][chunk:2 * chunk]
        q_dec.append((qs[u] * x_cum).astype(BF16))
        upds.append(jnp.where(state_mask, _dot_tn(vb, (ks[u] * x_rem).astype(BF16)), 0.0))
        decs.append(x_cum[chunk - 1:chunk, :])
    for r in range(n_rows):
        st = st_ref[r]
        for c in range(nc):
            u = r * nc + c
            sl = units[u][1]
            o = _dot_nt(q_dec[u], st.astype(BF16)) + o_intra[u]
            st = st * decs[u] + upds[u]
            gb = gb_ref[r, sl, :]
            for h in range(H_B):
                hs = slice(h * DV_B, (h + 1) * DV_B)
                gh = gb[:, hs]
                o_ref[r, sl, hs] = (_rms(o[:, hs], gout) * (gh * jax.nn.sigmoid(gh))).astype(o_ref.dtype)
        st_ref[r] = st

    @pl.when(j == pl.num_programs(1) - 1)
    def _():
        for r in range(n_rows):
            st = st_ref[r]
            s_t = st[0:DV_B]
            for h in range(1, H_B):
                s_t = s_t + st[h * DV_B:(h + 1) * DV_B]
            sfin_ref[r] = s_t.T.reshape(H_B, DK_B, DV_B)


def _gla(q, k, v, g, gb, s0, g_out, chunk, tb, rows):
    b, t, _ = q.shape
    m = _gla_decay_matrix(chunk)
    mdec = jnp.asarray(np.concatenate([m, m], axis=1), dtype=BF16)

    def tok(width):
        return pl.BlockSpec((rows, tb, width), lambda i, j: (i, j, 0))

    state_spec = pl.BlockSpec((rows, H_B, DK_B, DV_B), lambda i, j: (i, 0, 0, 0))
    return pl.pallas_call(
        functools.partial(_gla_kernel, chunk=chunk),
        grid=(b // rows, t // tb),
        in_specs=[tok(GK_W), tok(GK_W), tok(GV_W), tok(GK_W), tok(GV_W), state_spec,
                  pl.BlockSpec(mdec.shape, lambda i, j: (0, 0)),
                  pl.BlockSpec(g_out.shape, lambda i, j: (0, 0))],
        out_specs=[tok(GV_W), state_spec],
        out_shape=[jax.ShapeDtypeStruct((b, t, GV_W), BF16),
                   jax.ShapeDtypeStruct((b, H_B, DK_B, DV_B), F32)],
        scratch_shapes=[pltpu.VMEM((rows, GV_W, GK_W), F32)],
        compiler_params=pltpu.CompilerParams(
            dimension_semantics=("parallel", "arbitrary"), vmem_limit_bytes=VMEM_LIMIT_BYTES),
        name="gla",
    )(q, k, v, g, gb, s0, mdec, g_out)


def _rope_tables(pos):
    half = ROT_DIM // 2
    inv_freq = ROPE_THETA ** (-jnp.arange(half, dtype=F32) * (2.0 / ROT_DIM))
    ang = pos.astype(F32)[:, None] * inv_freq[None, :]
    cos, sin = jnp.cos(ang), jnp.sin(ang)
    n = pos.shape[0]
    ones = jnp.ones((n, HD_A - ROT_DIM), F32)
    zeros = jnp.zeros((n, HD_A - ROT_DIM), F32)
    zh = jnp.zeros((n, half), F32)
    c = jnp.concatenate([cos, cos, ones], axis=1)
    sa = jnp.concatenate([zh, sin, zeros], axis=1)
    sb = jnp.concatenate([-sin, zh, zeros], axis=1)
    rep = LANES // HD_A
    return tuple(jnp.tile(tab, (1, rep)) for tab in (c, sa, sb))


def _prep_weights(g_ffn1, w_ffn1_gu, w_ffn1_down, g_mix, w_in, w_gla_a2, b_gla_a, g_gla_out, w_out,
                  g_ffn2, w_ffn2_gu, w_ffn2_down, g_ple, w_ple_gate, w_ple_proj, g_final):
    a1_lo = 3 * ATT_W + 2 * GK_W + GV_W
    win = w_in[0]
    win = jnp.concatenate(
        [win[:, :a1_lo], win[:, a1_lo + GATE_RANK:], win[:, a1_lo:a1_lo + GATE_RANK],
         jnp.zeros((win.shape[0], LANES - GATE_RANK), win.dtype)], axis=1)
    wa2 = jnp.concatenate(
        [w_gla_a2[0], jnp.zeros((LANES - GATE_RANK, GK_W), w_gla_a2.dtype)], axis=0)
    return {
        "g_ffn1": g_ffn1, "w_ffn1_gu": w_ffn1_gu[0].astype(BF16),
        "w_ffn1_down": w_ffn1_down[0].astype(BF16), "g_mix": g_mix,
        "w_in": win.astype(BF16), "w_gla_a2": wa2.astype(BF16), "b_gla_a": b_gla_a,
        "g_gla_out": g_gla_out, "w_out": w_out[0].astype(BF16), "g_ffn2": g_ffn2,
        "w_ffn2_gu": w_ffn2_gu[0].astype(BF16), "w_ffn2_down": w_ffn2_down[0].astype(BF16),
        "g_ple": g_ple, "w_ple_gate": w_ple_gate[0].astype(BF16),
        "w_ple_proj": w_ple_proj[0].astype(BF16), "g_final": g_final.reshape(1, -1),
    }


def kernel(x_prompt, x_sample, cache_k_win, cache_v_win, state_gla, p_prompt, p_sample, g_ffn1, w_ffn1_gu, w_ffn1_down, g_mix, w_in, w_gla_a2, b_gla_a, g_gla_out, w_out, g_ffn2, w_ffn2_gu, w_ffn2_down, g_ple, w_ple_gate, w_ple_proj, g_final):
    assert w_in.shape[0] == 1, "single-layer step"
    w = _prep_weights(g_ffn1, w_ffn1_gu, w_ffn1_down, g_mix, w_in, w_gla_a2, b_gla_a, g_gla_out,
                      w_out, g_ffn2, w_ffn2_gu, w_ffn2_down, g_ple, w_ple_gate, w_ple_proj, g_final)
    bp, seq, d = x_prompt.shape
    bs, t_new, _ = x_sample.shape
    win_buf = cache_k_win.shape[2]
    assert seq == DILATED_BRANCHES[-1][0] == win_buf

    tm = 512
    rope_p = _rope_tables(jnp.arange(seq, dtype=jnp.int32))
    (h1, qa, ka, va, qb, kb, vb, gk, gb, kat, vat) = _pre_mixer(
        x_prompt.reshape(bp * seq, d), w, rope_p, tm, seq // tm)
    r3 = lambda a: a.reshape(bp, seq, a.shape[-1])
    att_p = _attn_prompt(r3(qa), r3(ka), r3(va))
    gla_p, state_p = _gla(r3(qb), r3(kb), r3(vb), r3(gk), r3(gb),
                          jnp.zeros((bp, H_B, DK_B, DV_B), F32), w["g_gla_out"],
                          chunk=GLA_CHUNK, tb=4 * GLA_CHUNK, rows=1)
    y_prompt = _post_mixer(h1, att_p.reshape(bp * seq, ATT_W), gla_p.reshape(bp * seq, GV_W),
                           p_prompt[0].reshape(bp * seq, -1), w, tm).reshape(bp, seq, d)
    win5 = lambda a: a.reshape(bp, H_A, HD_A, seq).transpose(0, 3, 1, 2)[None]
    k_win_prompt, v_win_prompt = win5(kat), win5(vat)

    ns = bs * t_new
    rope_s = _rope_tables(jnp.tile(PAST_LEN + jnp.arange(t_new, dtype=jnp.int32), bs))
    (h1s, qas, kas, vas, qbs, kbs, vbs, gks, gbs, _, _) = _pre_mixer(
        x_sample.reshape(ns, d), w, rope_s, ns, 1)
    r5 = lambda a: a.reshape(bs, t_new, H_A, HD_A)
    q4 = jnp.pad(r5(qas).transpose(0, 2, 1, 3), ((0, 0), (0, 0), (0, T_PAD - t_new), (0, 0)))
    new_cols = lambda a: jnp.pad(r5(a).transpose(0, 2, 3, 1),
                                 ((0, 0), (0, 0), (0, 0), (LANES - t_new, 0)))
    tok_last = lambda c: c[0].transpose(0, 2, 3, 1)
    att4, kto, vto = _attn_sample(q4, new_cols(kas), new_cols(vas), tok_last(cache_k_win),
                                  tok_last(cache_v_win), t_new)
    att_s = att4[:, :, :t_new].transpose(0, 2, 1, 3)
    k_win_sample = kto.transpose(0, 3, 1, 2)
    v_win_sample = vto.transpose(0, 3, 1, 2)
    s_chunk = 16
    pad = lambda a: jnp.pad(a.reshape(bs, t_new, a.shape[-1]), ((0, 0), (0, s_chunk - t_new), (0, 0)))
    gla_s, state_s = _gla(pad(qbs), pad(kbs), pad(vbs), pad(gks), pad(gbs), state_gla[0],
                          w["g_gla_out"], chunk=s_chunk, tb=s_chunk, rows=math.gcd(bs, 8))
    y_sample = _post_mixer(h1s, att_s.reshape(ns, ATT_W).astype(BF16),
                           gla_s[:, :t_new].reshape(ns, GV_W), p_sample[0].reshape(ns, -1),
                           w, ns).reshape(bs, t_new, d)

    return (y_prompt, y_sample, k_win_prompt, v_win_prompt, state_p[None],
            k_win_sample[None], v_win_sample[None], state_s[None])
```

```python
import functools
import math

import jax
import jax.numpy as jnp
import numpy as np
from jax import lax
from jax.experimental import pallas as pl
from jax.experimental.pallas import tpu as pltpu

F32 = jnp.float32
BF16 = jnp.bfloat16

H_A = 8
HD_A = 64
ATT_W = H_A * HD_A
DILATED_BRANCHES = ((128, 1), (512, 4), (2048, 16))
ROT_DIM = HD_A // 4
ROPE_THETA = 500000.0
PAST_LEN = 8192
H_B = 4
DK_B = 64
DV_B = 128
GK_W = H_B * DK_B
GV_W = H_B * DV_B
GATE_RANK = 16
GATE_NORM = 16.0
GLA_CHUNK = 64
EPS = 1e-6
NEG = -1e30

LANES = 128
MXU_TILE = 256
FFN_CHUNKS = 2
VMEM_LIMIT_BYTES = 56 * 1024 * 1024
ATT_BLK = 128
ATT_UNROLL = 8


def _rms(x, g):
    return x * lax.rsqrt(jnp.mean(x * x, axis=-1, keepdims=True) + EPS) * g


def _dot(a, b):
    return jnp.dot(a, b, preferred_element_type=F32)


def _dot_nt(a, b):
    return lax.dot_general(a, b, (((1,), (1,)), ((), ())), preferred_element_type=F32)


def _dot_tn(a, b):
    return lax.dot_general(a, b, (((0,), (0,)), ((), ())), preferred_element_type=F32)


def _swiglu_half_step(x, g_ref, wgu_ref, wdn_ref):
    d_ff = wdn_ref.shape[0]
    xn = _rms(x, g_ref[...]).astype(BF16)
    n_tiles = -(-d_ff // MXU_TILE)
    n_chunk = min(FFN_CHUNKS, n_tiles)
    edges = [min(d_ff, (n_tiles * c // n_chunk) * MXU_TILE) for c in range(n_chunk + 1)]
    acc = None
    for lo, hi in zip(edges[:-1], edges[1:]):
        gate = _dot(xn, wgu_ref[:, lo:hi])
        up = _dot(xn, wgu_ref[:, d_ff + lo:d_ff + hi])
        act = (gate * jax.nn.sigmoid(gate) * up).astype(BF16)
        part = _dot(act, wdn_ref[lo:hi, :])
        acc = part if acc is None else acc + part
    return x + 0.5 * acc


def _pre_mixer_kernel(x_ref, g1_ref, wgu_ref, wdn_ref, gm_ref, win_ref, wa2_ref, ba_ref,
                      rc_ref, rsa_ref, rsb_ref,
                      h1_ref, qa_ref, ka_ref, va_ref, qb_ref, kb_ref, vb_ref, gk_ref, gb_ref,
                      kat_ref, vat_ref):
    h1 = _swiglu_half_step(x_ref[...], g1_ref, wgu_ref, wdn_ref)
    h1_ref[...] = h1
    xm = _rms(h1, gm_ref[...]).astype(BF16)

    def proj(lo, width):
        return _dot_nt(xm, win_ref[lo:lo + width, :])

    rc, rsa, rsb = rc_ref[...], rsa_ref[...], rsb_ref[...]

    def rope(t, scale):
        cols = []
        for c in range(ATT_W // LANES):
            blk = t[:, c * LANES:(c + 1) * LANES]
            rot = (blk * rc + pltpu.roll(blk, ROT_DIM // 2, 1) * rsa
                   + pltpu.roll(blk, LANES - ROT_DIM // 2, 1) * rsb)
            cols.append(rot * scale if scale != 1.0 else rot)
        return jnp.concatenate(cols, axis=1)

    qa_ref[...] = rope(proj(0, ATT_W), HD_A ** -0.5)
    ka = rope(proj(ATT_W, ATT_W), 1.0)
    va = proj(2 * ATT_W, ATT_W)
    ka_ref[...] = ka
    va_ref[...] = va
    kat_ref[0] = ka.T
    vat_ref[0] = va.T
    off = 3 * ATT_W
    qb_ref[...] = proj(off, GK_W) * (DK_B ** -0.5)
    kb_ref[...] = proj(off + GK_W, GK_W)
    vb_ref[...] = proj(off + 2 * GK_W, GV_W)
    a1 = proj(off + 2 * GK_W + GV_W, GATE_RANK)
    gb_ref[...] = proj(off + 2 * GK_W + GV_W + GATE_RANK, GV_W)
    z = _dot(a1.astype(BF16), wa2_ref[...]) + ba_ref[...]
    gk_ref[...] = (jnp.minimum(z, 0.0) - jnp.log1p(jnp.exp(-jnp.abs(z)))) * (1.0 / GATE_NORM)


def _pre_mixer(x, w, rope_tabs, tm, rope_period_blocks):
    n, d = x.shape
    grid = (n // tm,)

    def tok(width):
        return pl.BlockSpec((tm, width), lambda i: (i, 0))

    def full(a):
        return pl.BlockSpec(a.shape, lambda i: (0,) * a.ndim)

    rope_spec = pl.BlockSpec((tm, LANES), lambda i: (i % rope_period_blocks, 0))
    weights = (w["g_ffn1"], w["w_ffn1_gu"], w["w_ffn1_down"], w["g_mix"], w["w_in"],
               w["w_gla_a2"], w["b_gla_a"])
    out_widths = (d, ATT_W, ATT_W, ATT_W, GK_W, GK_W, GV_W, GK_W, GV_W)
    tps = tm * rope_period_blocks
    tok_last = pl.BlockSpec((1, ATT_W, tm),
                            lambda i: (i // rope_period_blocks, 0, i % rope_period_blocks))
    tok_last_shape = jax.ShapeDtypeStruct((n // tps, ATT_W, tps), F32)
    return pl.pallas_call(
        _pre_mixer_kernel,
        grid=grid,
        in_specs=[tok(d)] + [full(a) for a in weights] + [rope_spec] * 3,
        out_specs=[tok(wd) for wd in out_widths] + [tok_last] * 2,
        out_shape=[jax.ShapeDtypeStruct((n, wd), F32) for wd in out_widths] + [tok_last_shape] * 2,
        compiler_params=pltpu.CompilerParams(
            dimension_semantics=("parallel",), vmem_limit_bytes=VMEM_LIMIT_BYTES),
        name="pre_mixer",
    )(x, *weights, *rope_tabs)


def _post_mixer_kernel(h1_ref, att_ref, gla_ref, p_ref, wout_ref, g2_ref, wgu_ref, wdn_ref,
                       gp_ref, wpg_ref, wpp_ref, gf_ref, y_ref):
    mix = _dot(att_ref[...], wout_ref[:ATT_W, :]) + _dot(gla_ref[...], wout_ref[ATT_W:, :])
    h2 = h1_ref[...] + mix
    h3 = _swiglu_half_step(h2, g2_ref, wgu_ref, wdn_ref)
    gate = jax.nn.sigmoid(_dot(_rms(h3, gp_ref[...]).astype(BF16), wpg_ref[...]))
    h4 = h3 + gate * _dot(p_ref[...].astype(BF16), wpp_ref[...])
    y_ref[...] = _rms(h4, gf_ref[...])


def _post_mixer(h1, att, gla, p, w, tm):
    n, d = h1.shape
    grid = (n // tm,)

    def tok(width):
        return pl.BlockSpec((tm, width), lambda i: (i, 0))

    def full(a):
        return pl.BlockSpec(a.shape, lambda i: (0,) * a.ndim)

    weights = (w["w_out"], w["g_ffn2"], w["w_ffn2_gu"], w["w_ffn2_down"], w["g_ple"],
               w["w_ple_gate"], w["w_ple_proj"], w["g_final"])
    return pl.pallas_call(
        _post_mixer_kernel,
        grid=grid,
        in_specs=[tok(d), tok(ATT_W), tok(GV_W), tok(p.shape[1])] + [full(a) for a in weights],
        out_specs=tok(d),
        out_shape=jax.ShapeDtypeStruct((n, d), F32),
        compiler_params=pltpu.CompilerParams(
            dimension_semantics=("parallel",), vmem_limit_bytes=VMEM_LIMIT_BYTES),
        name="post_mixer",
    )(h1, att, gla, p, *weights)


def _attn_prompt_kernel(q_ref, k_ref, v_ref, o_ref, oacc_ref, lse_ref):
    seq = q_ref.shape[1]
    qv, kv, vv = q_ref.at[0], k_ref.at[0], v_ref.at[0]
    blk = ATT_BLK
    lane = lax.broadcasted_iota(jnp.int32, (1, LANES), 1)
    head0 = lane < HD_A
    qi = lax.broadcasted_iota(jnp.int32, (blk, 2 * blk), 0)
    kj = lax.broadcasted_iota(jnp.int32, (blk, 2 * blk), 1)
    in_cur = (kj >= blk) & (kj - blk <= qi)
    in_prev = (kj < blk) & (qi <= kj)
    bias_first = jnp.where(in_cur, 0.0, NEG)
    bias_rest = jnp.where(in_cur | in_prev, 0.0, NEG)

    for bi, (window, dil) in enumerate(DILATED_BRANCHES):
        assert window // dil == blk
        nb = seq // (dil * blk)

        def unit(idx, dil=dil, nb=nb, bi=bi):
            r = idx // nb
            n = idx % nb
            start = r + n * (blk * dil)
            pstart = jnp.maximum(start - blk * dil, r)
            if dil == 1:
                start = pl.multiple_of(start, blk)
                pstart = pl.multiple_of(pstart, blk)

            def rows(ref, s0):
                if dil == 1:
                    return ref[pl.ds(s0, blk), :]
                return ref[pl.ds(s0, blk, stride=dil), :]

            qb = rows(qv, start)
            kcat = jnp.concatenate([rows(kv, pstart), rows(kv, start)], axis=0).astype(BF16)
            vcat = jnp.concatenate([rows(vv, pstart), rows(vv, start)], axis=0).astype(BF16)
            bias = jnp.where(n > 0, bias_rest, bias_first)
            outs, lses = [], []
            for hmask in (head0, ~head0):
                qh = jnp.where(hmask, qb, 0.0).astype(BF16)
                s = _dot_nt(qh, kcat) + bias
                m = jnp.max(s, axis=-1, keepdims=True)
                p = jnp.exp(s - m)
                den = jnp.sum(p, axis=-1, keepdims=True)
                outs.append(_dot(p.astype(BF16), vcat) / den)
                lses.append(m + jnp.log(den))
            o = jnp.where(head0, outs[0], outs[1])
            l = jnp.where(head0, lses[0], lses[1])
            if dil == 1:
                oacc_ref[bi, pl.ds(start, blk), :] = o
                lse_ref[bi, pl.ds(start, blk), :] = l
            else:
                oacc_ref.at[bi][pl.ds(start, blk, stride=dil), :] = o
                lse_ref.at[bi][pl.ds(start, blk, stride=dil), :] = l

        def body(it, carry, unit=unit):
            for u in range(ATT_UNROLL):
                unit(it * ATT_UNROLL + u)
            return carry

        lax.fori_loop(0, seq // (blk * ATT_UNROLL), body, 0)

    rows_per = 256
    for c in range(seq // rows_per):
        sl = slice(c * rows_per, (c + 1) * rows_per)
        l0, l1, l2 = lse_ref[0, sl, :], lse_ref[1, sl, :], lse_ref[2, sl, :]
        mx = jnp.maximum(jnp.maximum(l0, l1), l2)
        w0, w1, w2 = jnp.exp(l0 - mx), jnp.exp(l1 - mx), jnp.exp(l2 - mx)
        tot = w0 + w1 + w2
        num = w0 * oacc_ref[0, sl, :] + w1 * oacc_ref[1, sl, :] + w2 * oacc_ref[2, sl, :]
        o_ref[0, sl, :] = (num / tot).astype(o_ref.dtype)


def _attn_prompt(q, k, v):
    b, seq, width = q.shape
    spec = pl.BlockSpec((1, seq, LANES), lambda i, j: (i, 0, j))
    nbr = len(DILATED_BRANCHES)
    return pl.pallas_call(
        _attn_prompt_kernel,
        grid=(b, width // LANES),
        in_specs=[spec] * 3,
        out_specs=spec,
        out_shape=jax.ShapeDtypeStruct((b, seq, width), BF16),
        scratch_shapes=[pltpu.VMEM((nbr, seq, LANES), F32), pltpu.VMEM((nbr, seq, LANES), F32)],
        compiler_params=pltpu.CompilerParams(
            dimension_semantics=("parallel", "parallel"), vmem_limit_bytes=VMEM_LIMIT_BYTES),
        name="attn_prompt",
    )(q, k, v)


T_PAD = 8


def _attn_sample_kernel(q_ref, knt_ref, vnt_ref, kt_ref, vt_ref, att_ref, ko_ref, vo_ref,
                        knr_ref, vnr_ref, *, t_new):
    n_heads, dh, w = kt_ref.shape[1], kt_ref.shape[2], kt_ref.shape[3]
    row = pl.program_id(0)

    @pl.when(row == 0)
    def _():
        for b in range(knr_ref.shape[0]):
            shift = LANES - t_new - b * t_new
            knr_ref[b] = pltpu.roll(knt_ref[0], shift, 1)
            vnr_ref[b] = pltpu.roll(vnt_ref[0], shift, 1)
    tq = lax.broadcasted_iota(jnp.int32, (T_PAD, w), 0)
    tok = lax.broadcasted_iota(jnp.int32, (T_PAD, w), 1)
    tqn = lax.broadcasted_iota(jnp.int32, (T_PAD, LANES), 0)
    un = lax.broadcasted_iota(jnp.int32, (T_PAD, LANES), 1) - (LANES - t_new)
    lane_new = lax.broadcasted_iota(jnp.int32, (1, LANES), 1) >= LANES - t_new
    cache_masks, new_masks = [], []
    for window, dil in DILATED_BRANCHES:
        assert w % dil == 0 and dil & (dil - 1) == 0
        cache_masks.append((tq < t_new) & (tok >= w + tq - window)
                           & (((tok - tq) & (dil - 1)) == 0))
        new_masks.append((tqn < t_new) & (un >= 0) & (un <= tqn)
                         & (((tqn - un) & (dil - 1)) == 0))

    for h in range(n_heads):
        q = q_ref[0, h].astype(BF16)
        kt, vt = kt_ref[0, h], vt_ref[0, h]
        knp = knr_ref[row, h * dh:(h + 1) * dh, :]
        vnp = vnr_ref[row, h * dh:(h + 1) * dh, :]
        sc = _dot(q, kt.astype(BF16))
        sn = _dot(q, knp.astype(BF16))
        pcs, pns, dens, lses = [], [], [], []
        for cm, nm in zip(cache_masks, new_masks):
            scm = jnp.where(cm, sc, NEG)
            snm = jnp.where(nm, sn, NEG)
            m = jnp.maximum(jnp.max(scm, axis=-1, keepdims=True), jnp.max(snm, axis=-1, keepdims=True))
            pc, pn = jnp.exp(scm - m), jnp.exp(snm - m)
            den = jnp.sum(pc, axis=-1, keepdims=True) + jnp.sum(pn, axis=-1, keepdims=True)
            pcs.append(pc)
            pns.append(pn)
            dens.append(den)
            lses.append(m + jnp.log(den))
        o_all = (_dot_nt(jnp.concatenate(pcs, axis=0).astype(BF16), vt.astype(BF16))
                 + _dot_nt(jnp.concatenate(pns, axis=0).astype(BF16), vnp.astype(BF16)))
        mx = jnp.maximum(jnp.maximum(lses[0], lses[1]), lses[2])
        num, tot = 0.0, 0.0
        for bi in range(len(DILATED_BRANCHES)):
            wgt = jnp.exp(lses[bi] - mx)
            num = num + wgt * (o_all[bi * T_PAD:(bi + 1) * T_PAD] / dens[bi])
            tot = tot + wgt
        att_ref[0, h] = num / tot
        for src, newp, dst in ((kt, knp, ko_ref), (vt, vnp, vo_ref)):
            rolled = pltpu.roll(src, w - t_new, 1)
            dst[0, h, :, :w - LANES] = rolled[:, :w - LANES]
            dst[0, h, :, w - LANES:] = jnp.where(lane_new, newp, rolled[:, w - LANES:])


def _attn_sample(q4, knt, vnt, kt, vt, t_new):
    b, h, _, dh = q4.shape
    w = kt.shape[3]
    assert knt.shape == (1, h * dh, LANES) and b * t_new == LANES
    q_spec = pl.BlockSpec((1, h, T_PAD, dh), lambda i: (i, 0, 0, 0))
    new_spec = pl.BlockSpec(knt.shape, lambda i: (0, 0, 0))
    cache_spec = pl.BlockSpec((1, h, dh, w), lambda i: (i, 0, 0, 0))
    return pl.pallas_call(
        functools.partial(_attn_sample_kernel, t_new=t_new),
        grid=(b,),
        in_specs=[q_spec, new_spec, new_spec, cache_spec, cache_spec],
        out_specs=[q_spec, cache_spec, cache_spec],
        out_shape=[jax.ShapeDtypeStruct(q4.shape, F32), jax.ShapeDtypeStruct(kt.shape, F32),
                   jax.ShapeDtypeStruct(vt.shape, F32)],
        scratch_shapes=[pltpu.VMEM((b, h * dh, LANES), F32), pltpu.VMEM((b, h * dh, LANES), F32)],
        compiler_params=pltpu.CompilerParams(
            dimension_semantics=("arbitrary",), vmem_limit_bytes=VMEM_LIMIT_BYTES),
        name="attn_sample",
    )(q4, knt, vnt, kt, vt)


def _gla_decay_matrix(chunk):
    t = np.arange(chunk)[:, None]
    u = np.arange(chunk)[None, :]
    blocks = [u <= t, u > t]
    s = chunk // 2
    while s >= 1:
        mid = (t // (2 * s)) * (2 * s) + s
        upper = (t >= mid) & (u >= mid) & (u <= t)
        lower = (t < mid) & (u > t) & (u < mid)
        blocks.append(upper | lower)
        s //= 2
    return np.concatenate(blocks, axis=0).astype(np.float32)


def _gla_kernel(q_ref, k_ref, v_ref, g_ref, gb_ref, s0_ref, m_ref, gout_ref,
                o_ref, sfin_ref, st_ref, *, chunk):
    j = pl.program_id(1)
    n_lvl = int(math.log2(chunk))
    n_rows, tb = q_ref.shape[0], q_ref.shape[1]
    nc = tb // chunk
    hk = lax.broadcasted_iota(jnp.int32, (1, GK_W), 1) // DK_B
    hv = lax.broadcasted_iota(jnp.int32, (1, GV_W), 1) // DV_B
    hv_rows = lax.broadcasted_iota(jnp.int32, (GV_W, 1), 0) // DV_B
    state_mask = hv_rows == hk

    @pl.when(j == 0)
    def _():
        for r in range(n_rows):
            s0t = s0_ref[r].reshape(GK_W, DV_B).T
            st_ref[r] = jnp.where(state_mask, jnp.concatenate([s0t] * H_B, axis=0), 0.0)

    pi = lax.broadcasted_iota(jnp.int32, (chunk, H_B * chunk), 0)
    pj = lax.broadcasted_iota(jnp.int32, (chunk, H_B * chunk), 1) % chunk
    pair_masks = []
    s = chunk // 2
    while s >= 1:
        same = (pi // (2 * s)) == (pj // (2 * s))
        pair_masks.append(same & (pi % (2 * s) >= s) & (pj % (2 * s) < s))
        s //= 2
    pair_masks.append(pi == pj)
    a_head = lax.broadcasted_iota(jnp.int32, (H_B * chunk, 1), 0) // chunk
    bd_k = jnp.where(a_head == hk, 1.0, 0.0).astype(BF16)
    bd_v = jnp.where(a_head == hv, 1.0, 0.0).astype(BF16)

    mdec2 = m_ref[...]
    gout = gout_ref[...]
    units = [(r, slice(c * chunk, (c + 1) * chunk)) for r in range(n_rows) for c in range(nc)]
    qs = [q_ref[r, sl, :] for r, sl in units]
    ks = [k_ref[r, sl, :] for r, sl in units]
    xs = []
    for r, sl in units:
        g = g_ref[r, sl, :]
        g_hi = g.astype(BF16)
        g_lo = (g - g_hi.astype(F32)).astype(BF16)
        xs.append(jnp.exp(_dot(mdec2, jnp.concatenate([g_hi, g_lo], axis=0))))
    a_alls = [jnp.zeros((chunk, H_B * chunk), F32)] * len(units)
    for lvl in range(n_lvl + 1):
        for u in range(len(units)):
            if lvl < n_lvl:
                xl = xs[u][(2 + lvl) * chunk:(3 + lvl) * chunk]
                ql, kl = (qs[u] * xl).astype(BF16), (ks[u] * xl).astype(BF16)
            else:
                ql, kl = qs[u].astype(BF16), ks[u].astype(BF16)
            kl_bd = jnp.concatenate([kl] * H_B, axis=0) * bd_k
            a_alls[u] = jnp.where(pair_masks[lvl], _dot_nt(ql, kl_bd), a_alls[u])
    o_intra, q_dec, upds, decs = [], [], [], []
    for u, (r, sl) in enumerate(units):
        vb = v_ref[r, sl, :].astype(BF16)
        v_bd = jnp.concatenate([vb] * H_B, axis=0) * bd_v
        o_intra.append(_dot(a_alls[u].astype(BF16), v_bd))
        x_cum, x_rem = xs[u][0:chunk], xs[u][chunk:2 * chunk]
        q_dec.append((qs[u] * x_cum).astype(BF16))
        upds.append(jnp.where(state_mask, _dot_tn(vb, (ks[u] * x_rem).astype(BF16)), 0.0))
        decs.append(x_cum[chunk - 1:chunk, :])
    for r in range(n_rows):
        st = st_ref[r]
        for c in range(nc):
            u = r * nc + c
            sl = units[u][1]
            o = _dot_nt(q_dec[u], st.astype(BF16)) + o_intra[u]
            st = st * decs[u] + upds[u]
            gb = gb_ref[r, sl, :]
            for h in range(H_B):
                hs = slice(h * DV_B, (h + 1) * DV_B)
                gh = gb[:, hs]
                o_ref[r, sl, hs] = (_rms(o[:, hs], gout) * (gh * jax.nn.sigmoid(gh))).astype(o_ref.dtype)
        st_ref[r] = st

    @pl.when(j == pl.num_programs(1) - 1)
    def _():
        for r in range(n_rows):
            st = st_ref[r]
            s_t = st[0:DV_B]
            for h in range(1, H_B):
                s_t = s_t + st[h * DV_B:(h + 1) * DV_B]
            sfin_ref[r] = s_t.T.reshape(H_B, DK_B, DV_B)


def _gla(q, k, v, g, gb, s0, g_out, chunk, tb, rows):
    b, t, _ = q.shape
    m = _gla_decay_matrix(chunk)
    mdec = jnp.asarray(np.concatenate([m, m], axis=1), dtype=BF16)

    def tok(width):
        return pl.BlockSpec((rows, tb, width), lambda i, j: (i, j, 0))

    state_spec = pl.BlockSpec((rows, H_B, DK_B, DV_B), lambda i, j: (i, 0, 0, 0))
    return pl.pallas_call(
        functools.partial(_gla_kernel, chunk=chunk),
        grid=(b // rows, t // tb),
        in_specs=[tok(GK_W), tok(GK_W), tok(GV_W), tok(GK_W), tok(GV_W), state_spec,
                  pl.BlockSpec(mdec.shape, lambda i, j: (0, 0)),
                  pl.BlockSpec(g_out.shape, lambda i, j: (0, 0))],
        out_specs=[tok(GV_W), state_spec],
        out_shape=[jax.ShapeDtypeStruct((b, t, GV_W), BF16),
                   jax.ShapeDtypeStruct((b, H_B, DK_B, DV_B), F32)],
        scratch_shapes=[pltpu.VMEM((rows, GV_W, GK_W), F32)],
        compiler_params=pltpu.CompilerParams(
            dimension_semantics=("parallel", "arbitrary"), vmem_limit_bytes=VMEM_LIMIT_BYTES),
        name="gla",
    )(q, k, v, g, gb, s0, mdec, g_out)


def _rope_tables(pos):
    half = ROT_DIM // 2
    inv_freq = ROPE_THETA ** (-jnp.arange(half, dtype=F32) * (2.0 / ROT_DIM))
    ang = pos.astype(F32)[:, None] * inv_freq[None, :]
    cos, sin = jnp.cos(ang), jnp.sin(ang)
    n = pos.shape[0]
    ones = jnp.ones((n, HD_A - ROT_DIM), F32)
    zeros = jnp.zeros((n, HD_A - ROT_DIM), F32)
    zh = jnp.zeros((n, half), F32)
    c = jnp.concatenate([cos, cos, ones], axis=1)
    sa = jnp.concatenate([zh, sin, zeros], axis=1)
    sb = jnp.concatenate([-sin, zh, zeros], axis=1)
    rep = LANES // HD_A
    return tuple(jnp.tile(tab, (1, rep)) for tab in (c, sa, sb))


def _prep_weights(g_ffn1, w_ffn1_gu, w_ffn1_down, g_mix, w_in, w_gla_a2, b_gla_a, g_gla_out, w_out,
                  g_ffn2, w_ffn2_gu, w_ffn2_down, g_ple, w_ple_gate, w_ple_proj, g_final):
    win = w_in[0].T
    wa2 = w_gla_a2[0]
    return {
        "g_ffn1": g_ffn1, "w_ffn1_gu": w_ffn1_gu[0].astype(BF16),
        "w_ffn1_down": w_ffn1_down[0].astype(BF16), "g_mix": g_mix,
        "w_in": win.astype(BF16), "w_gla_a2": wa2.astype(BF16), "b_gla_a": b_gla_a,
        "g_gla_out": g_gla_out, "w_out": w_out[0].astype(BF16), "g_ffn2": g_ffn2,
        "w_ffn2_gu": w_ffn2_gu[0].astype(BF16), "w_ffn2_down": w_ffn2_down[0].astype(BF16),
        "g_ple": g_ple, "w_ple_gate": w_ple_gate[0].astype(BF16),
        "w_ple_proj": w_ple_proj[0].astype(BF16), "g_final": g_final.reshape(1, -1),
    }


def kernel(x_prompt, x_sample, cache_k_win, cache_v_win, state_gla, p_prompt, p_sample, g_ffn1, w_ffn1_gu, w_ffn1_down, g_mix, w_in, w_gla_a2, b_gla_a, g_gla_out, w_out, g_ffn2, w_ffn2_gu, w_ffn2_down, g_ple, w_ple_gate, w_ple_proj, g_final):
    assert w_in.shape[0] == 1, "single-layer step"
    w = _prep_weights(g_ffn1, w_ffn1_gu, w_ffn1_down, g_mix, w_in, w_gla_a2, b_gla_a, g_gla_out,
                      w_out, g_ffn2, w_ffn2_gu, w_ffn2_down, g_ple, w_ple_gate, w_ple_proj, g_final)
    bp, seq, d = x_prompt.shape
    bs, t_new, _ = x_sample.shape
    win_buf = cache_k_win.shape[2]
    assert seq == DILATED_BRANCHES[-1][0] == win_buf

    tm = 512
    rope_p = _rope_tables(jnp.arange(seq, dtype=jnp.int32))
    (h1, qa, ka, va, qb, kb, vb, gk, gb, kat, vat) = _pre_mixer(
        x_prompt.reshape(bp * seq, d), w, rope_p, tm, seq // tm)
    r3 = lambda a: a.reshape(bp, seq, a.shape[-1])
    att_p = _attn_prompt(r3(qa), r3(ka), r3(va))
    gla_p, state_p = _gla(r3(qb), r3(kb), r3(vb), r3(gk), r3(gb),
                          jnp.zeros((bp, H_B, DK_B, DV_B), F32), w["g_gla_out"],
                          chunk=GLA_CHUNK, tb=8 * GLA_CHUNK, rows=1)
    y_prompt = _post_mixer(h1, att_p.reshape(bp * seq, ATT_W), gla_p.reshape(bp * seq, GV_W),
                           p_prompt[0].reshape(bp * seq, -1), w, tm).reshape(bp, seq, d)
    win5 = lambda a: a.reshape(bp, H_A, HD_A, seq).transpose(0, 3, 1, 2)[None]
    k_win_prompt, v_win_prompt = win5(kat), win5(vat)

    ns = bs * t_new
    rope_s = _rope_tables(jnp.tile(PAST_LEN + jnp.arange(t_new, dtype=jnp.int32), bs))
    (h1s, qas, _, _, qbs, kbs, vbs, gks, gbs, kats, vats) = _pre_mixer(
        x_sample.reshape(ns, d), w, rope_s, ns, 1)
    q4 = jnp.pad(qas.reshape(bs, t_new, H_A, HD_A).transpose(0, 2, 1, 3),
                 ((0, 0), (0, 0), (0, T_PAD - t_new), (0, 0)))
    tok_last = lambda c: c[0].transpose(0, 2, 3, 1)
    att4, kto, vto = _attn_sample(q4, kats, vats, tok_last(cache_k_win), tok_last(cache_v_win),
                                  t_new)
    att_s = att4[:, :, :t_new].transpose(0, 2, 1, 3)
    k_win_sample = kto.transpose(0, 3, 1, 2)
    v_win_sample = vto.transpose(0, 3, 1, 2)
    s_chunk = 16
    pad = lambda a: jnp.pad(a.reshape(bs, t_new, a.shape[-1]), ((0, 0), (0, s_chunk - t_new), (0, 0)))
    gla_s, state_s = _gla(pad(qbs), pad(kbs), pad(vbs), pad(gks), pad(gbs), state_gla[0],
                          w["g_gla_out"], chunk=s_chunk, tb=s_chunk, rows=math.gcd(bs, 8))
    y_sample = _post_mixer(h1s, att_s.reshape(ns, ATT_W).astype(BF16),
                           gla_s[:, :t_new].reshape(ns, GV_W), p_sample[0].reshape(ns, -1),
                           w, ns).reshape(bs, t_new, d)

    return (y_prompt, y_sample, k_win_prompt, v_win_prompt, state_p[None],
            k_win_sample[None], v_win_sample[None], state_s[None])
```

```python
import functools
import math

import jax
import jax.numpy as jnp
import numpy as np
from jax import lax
from jax.experimental import pallas as pl
from jax.experimental.pallas import tpu as pltpu

F32 = jnp.float32
BF16 = jnp.bfloat16

H_A = 8
HD_A = 64
ATT_W = H_A * HD_A
DILATED_BRANCHES = ((128, 1), (512, 4), (2048, 16))
ROT_DIM = HD_A // 4
ROPE_THETA = 500000.0
PAST_LEN = 8192
H_B = 4
DK_B = 64
DV_B = 128
GK_W = H_B * DK_B
GV_W = H_B * DV_B
GATE_RANK = 16
GATE_NORM = 16.0
GLA_CHUNK = 64
EPS = 1e-6
NEG = -1e30

LANES = 128
MXU_TILE = 256
FFN_CHUNKS = 4
ROW_GROUPS = 2
VMEM_LIMIT_BYTES = 56 * 1024 * 1024
ATT_BLK = 128
ATT_UNROLL = 16


def _rms(x, g):
    return x * lax.rsqrt(jnp.mean(x * x, axis=-1, keepdims=True) + EPS) * g


def _dot(a, b):
    return jnp.dot(a, b, preferred_element_type=F32)


def _dot_nt(a, b):
    return lax.dot_general(a, b, (((1,), (1,)), ((), ())), preferred_element_type=F32)


def _dot_tn(a, b):
    return lax.dot_general(a, b, (((0,), (0,)), ((), ())), preferred_element_type=F32)


def _row_groups(ref_rows):
    n = ROW_GROUPS if ref_rows % (ROW_GROUPS * 16) == 0 else 1
    step = ref_rows // n
    return [slice(i * step, (i + 1) * step) for i in range(n)]


def _swiglu_half_step(xs, g_ref, wgu_ref, wdn_ref):
    d_ff = wdn_ref.shape[0]
    xns = [_rms(x, g_ref[...]).astype(BF16) for x in xs]
    n_tiles = -(-d_ff // MXU_TILE)
    n_chunk = min(FFN_CHUNKS, n_tiles)
    edges = [min(d_ff, (n_tiles * c // n_chunk) * MXU_TILE) for c in range(n_chunk + 1)]
    accs = [None] * len(xs)
    for lo, hi in zip(edges[:-1], edges[1:]):
        for i, xn in enumerate(xns):
            gate = _dot(xn, wgu_ref[:, lo:hi])
            up = _dot(xn, wgu_ref[:, d_ff + lo:d_ff + hi])
            act = (gate * jax.nn.sigmoid(gate) * up).astype(BF16)
            part = _dot(act, wdn_ref[lo:hi, :])
            accs[i] = part if accs[i] is None else accs[i] + part
    return [x + 0.5 * acc for x, acc in zip(xs, accs)]


def _pre_mixer_kernel(x_ref, g1_ref, wgu_ref, wdn_ref, gm_ref, win_ref, wa2_ref, ba_ref,
                      rc_ref, rsa_ref, rsb_ref,
                      h1_ref, qa_ref, ka_ref, va_ref, qb_ref, kb_ref, vb_ref, gk_ref, gb_ref,
                      kat_ref, vat_ref):
    groups = _row_groups(x_ref.shape[0])
    h1s = _swiglu_half_step([x_ref[sl, :] for sl in groups], g1_ref, wgu_ref, wdn_ref)
    for sl, h1 in zip(groups, h1s):
        h1_ref[sl, :] = h1
    xms = [_rms(h1, gm_ref[...]).astype(BF16) for h1 in h1s]

    def rope(t, sl, scale):
        rc, rsa, rsb = rc_ref[sl, :], rsa_ref[sl, :], rsb_ref[sl, :]
        cols = []
        for c in range(ATT_W // LANES):
            blk = t[:, c * LANES:(c + 1) * LANES]
            rot = (blk * rc + pltpu.roll(blk, ROT_DIM // 2, 1) * rsa
                   + pltpu.roll(blk, LANES - ROT_DIM // 2, 1) * rsb)
            cols.append(rot * scale if scale != 1.0 else rot)
        return jnp.concatenate(cols, axis=1)

    off = 3 * ATT_W
    for sl, xm in zip(groups, xms):
        def proj(lo, width, xm=xm):
            return _dot_nt(xm, win_ref[lo:lo + width, :])

        qa_ref[sl, :] = rope(proj(0, ATT_W), sl, HD_A ** -0.5)
        ka = rope(proj(ATT_W, ATT_W), sl, 1.0)
        va = proj(2 * ATT_W, ATT_W)
        ka_ref[sl, :] = ka
        va_ref[sl, :] = va
        kat_ref[0, :, sl] = ka.T
        vat_ref[0, :, sl] = va.T
        qb_ref[sl, :] = proj(off, GK_W) * (DK_B ** -0.5)
        kb_ref[sl, :] = proj(off + GK_W, GK_W)
        vb_ref[sl, :] = proj(off + 2 * GK_W, GV_W)
        a1 = proj(off + 2 * GK_W + GV_W, GATE_RANK)
        gb_ref[sl, :] = proj(off + 2 * GK_W + GV_W + GATE_RANK, GV_W)
        z = _dot(a1.astype(BF16), wa2_ref[...]) + ba_ref[...]
        gk_ref[sl, :] = (jnp.minimum(z, 0.0) - jnp.log1p(jnp.exp(-jnp.abs(z)))) * (1.0 / GATE_NORM)


def _pre_mixer(x, w, rope_tabs, tm, rope_period_blocks):
    n, d = x.shape
    grid = (n // tm,)

    def tok(width):
        return pl.BlockSpec((tm, width), lambda i: (i, 0))

    def full(a):
        return pl.BlockSpec(a.shape, lambda i: (0,) * a.ndim)

    rope_spec = pl.BlockSpec((tm, LANES), lambda i: (i % rope_period_blocks, 0))
    weights = (w["g_ffn1"], w["w_ffn1_gu"], w["w_ffn1_down"], w["g_mix"], w["w_in"],
               w["w_gla_a2"], w["b_gla_a"])
    out_widths = (d, ATT_W, ATT_W, ATT_W, GK_W, GK_W, GV_W, GK_W, GV_W)
    tps = tm * rope_period_blocks
    tok_last = pl.BlockSpec((1, ATT_W, tm),
                            lambda i: (i // rope_period_blocks, 0, i % rope_period_blocks))
    tok_last_shape = jax.ShapeDtypeStruct((n // tps, ATT_W, tps), F32)
    return pl.pallas_call(
        _pre_mixer_kernel,
        grid=grid,
        in_specs=[tok(d)] + [full(a) for a in weights] + [rope_spec] * 3,
        out_specs=[tok(wd) for wd in out_widths] + [tok_last] * 2,
        out_shape=[jax.ShapeDtypeStruct((n, wd), F32) for wd in out_widths] + [tok_last_shape] * 2,
        compiler_params=pltpu.CompilerParams(
            dimension_semantics=("parallel",), vmem_limit_bytes=VMEM_LIMIT_BYTES),
        name="pre_mixer",
    )(x, *weights, *rope_tabs)


def _post_mixer_kernel(h1_ref, att_ref, gla_ref, p_ref, wout_ref, g2_ref, wgu_ref, wdn_ref,
                       gp_ref, wpg_ref, wpp_ref, gf_ref, y_ref):
    groups = _row_groups(h1_ref.shape[0])
    h2s = [h1_ref[sl, :] + _dot(att_ref[sl, :], wout_ref[:ATT_W, :])
           + _dot(gla_ref[sl, :], wout_ref[ATT_W:, :]) for sl in groups]
    h3s = _swiglu_half_step(h2s, g2_ref, wgu_ref, wdn_ref)
    for sl, h3 in zip(groups, h3s):
        gate = jax.nn.sigmoid(_dot(_rms(h3, gp_ref[...]).astype(BF16), wpg_ref[...]))
        h4 = h3 + gate * _dot(p_ref[sl, :].astype(BF16), wpp_ref[...])
        y_ref[sl, :] = _rms(h4, gf_ref[...])


def _post_mixer(h1, att, gla, p, w, tm):
    n, d = h1.shape
    grid = (n // tm,)

    def tok(width):
        return pl.BlockSpec((tm, width), lambda i: (i, 0))

    def full(a):
        return pl.BlockSpec(a.shape, lambda i: (0,) * a.ndim)

    weights = (w["w_out"], w["g_ffn2"], w["w_ffn2_gu"], w["w_ffn2_down"], w["g_ple"],
               w["w_ple_gate"], w["w_ple_proj"], w["g_final"])
    return pl.pallas_call(
        _post_mixer_kernel,
        grid=grid,
        in_specs=[tok(d), tok(ATT_W), tok(GV_W), tok(p.shape[1])] + [full(a) for a in weights],
        out_specs=tok(d),
        out_shape=jax.ShapeDtypeStruct((n, d), F32),
        compiler_params=pltpu.CompilerParams(
            dimension_semantics=("parallel",), vmem_limit_bytes=VMEM_LIMIT_BYTES),
        name="post_mixer",
    )(h1, att, gla, p, *weights)


def _attn_prompt_kernel(q_ref, k_ref, v_ref, o_ref, oacc_ref, max_ref, den_ref):
    seq = q_ref.shape[1]
    qv, kv, vv = q_ref.at[0], k_ref.at[0], v_ref.at[0]
    blk = ATT_BLK
    lane = lax.broadcasted_iota(jnp.int32, (1, LANES), 1)
    head0 = lane < HD_A
    qi = lax.broadcasted_iota(jnp.int32, (blk, 2 * blk), 0)
    kj = lax.broadcasted_iota(jnp.int32, (blk, 2 * blk), 1)
    in_cur = (kj >= blk) & (kj - blk <= qi)
    in_prev = (kj < blk) & (qi <= kj)
    bias_first = jnp.where(in_cur, 0.0, NEG)
    bias_rest = jnp.where(in_cur | in_prev, 0.0, NEG)

    for bi, (window, dil) in enumerate(DILATED_BRANCHES):
        assert window // dil == blk
        nb = seq // (dil * blk)

        def unit(idx, dil=dil, nb=nb, bi=bi):
            r = idx // nb
            n = idx % nb
            start = r + n * (blk * dil)
            pstart = jnp.maximum(start - blk * dil, r)
            if dil == 1:
                start = pl.multiple_of(start, blk)
                pstart = pl.multiple_of(pstart, blk)

            def rows(ref, s0):
                if dil == 1:
                    return ref[pl.ds(s0, blk), :]
                return ref[pl.ds(s0, blk, stride=dil), :]

            qb = rows(qv, start)
            if nb == 1:
                none = jnp.zeros((blk, LANES), F32)
                kcat = jnp.concatenate([none, rows(kv, start)], axis=0).astype(BF16)
                vcat = jnp.concatenate([none, rows(vv, start)], axis=0).astype(BF16)
                bias = bias_first
            else:
                kcat = jnp.concatenate([rows(kv, pstart), rows(kv, start)], axis=0).astype(BF16)
                vcat = jnp.concatenate([rows(vv, pstart), rows(vv, start)], axis=0).astype(BF16)
                bias = jnp.where(n > 0, bias_rest, bias_first)
            outs, maxes, dens = [], [], []
            for hmask in (head0, ~head0):
                qh = jnp.where(hmask, qb, 0.0).astype(BF16)
                s = _dot_nt(qh, kcat) + bias
                m = jnp.max(s, axis=-1, keepdims=True)
                p = jnp.exp(s - m)
                maxes.append(m)
                dens.append(jnp.sum(p, axis=-1, keepdims=True))
                outs.append(_dot(p.astype(BF16), vcat))
            stats = (jnp.where(head0, outs[0], outs[1]), jnp.where(head0, maxes[0], maxes[1]),
                     jnp.where(head0, dens[0], dens[1]))
            for ref, val in zip((oacc_ref, max_ref, den_ref), stats):
                if dil == 1:
                    ref[bi, pl.ds(start, blk), :] = val
                else:
                    ref.at[bi][pl.ds(start, blk, stride=dil), :] = val

        def body(it, carry, unit=unit):
            for u in range(ATT_UNROLL):
                unit(it * ATT_UNROLL + u)
            return carry

        lax.fori_loop(0, seq // (blk * ATT_UNROLL), body, 0)

    rows_per = 256
    nbr = len(DILATED_BRANCHES)
    for c in range(seq // rows_per):
        sl = slice(c * rows_per, (c + 1) * rows_per)
        ms = [max_ref[b, sl, :] for b in range(nbr)]
        top = functools.reduce(jnp.maximum, ms)
        es = [jnp.exp(m - top) for m in ms]
        num = sum(e * oacc_ref[b, sl, :] for b, e in enumerate(es))
        tot = sum(e * den_ref[b, sl, :] for b, e in enumerate(es))
        o_ref[0, sl, :] = (num / tot).astype(o_ref.dtype)


def _attn_prompt(q, k, v):
    b, seq, width = q.shape
    spec = pl.BlockSpec((1, seq, LANES), lambda i, j: (i, 0, j))
    nbr = len(DILATED_BRANCHES)
    return pl.pallas_call(
        _attn_prompt_kernel,
        grid=(b, width // LANES),
        in_specs=[spec] * 3,
        out_specs=spec,
        out_shape=jax.ShapeDtypeStruct((b, seq, width), BF16),
        scratch_shapes=[pltpu.VMEM((nbr, seq, LANES), F32)] * 3,
        compiler_params=pltpu.CompilerParams(
            dimension_semantics=("parallel", "parallel"), vmem_limit_bytes=VMEM_LIMIT_BYTES),
        name="attn_prompt",
    )(q, k, v)


T_PAD = 8


def _attn_sample_kernel(q_ref, knt_ref, vnt_ref, kt_ref, vt_ref, att_ref, ko_ref, vo_ref,
                        knr_ref, vnr_ref, *, t_new):
    n_heads, dh, w = kt_ref.shape[1], kt_ref.shape[2], kt_ref.shape[3]
    row = pl.program_id(0)

    @pl.when(row == 0)
    def _():
        for b in range(knr_ref.shape[0]):
            shift = LANES - t_new - b * t_new
            knr_ref[b] = pltpu.roll(knt_ref[0], shift, 1)
            vnr_ref[b] = pltpu.roll(vnt_ref[0], shift, 1)
    tq = lax.broadcasted_iota(jnp.int32, (T_PAD, w), 0)
    tok = lax.broadcasted_iota(jnp.int32, (T_PAD, w), 1)
    tqn = lax.broadcasted_iota(jnp.int32, (T_PAD, LANES), 0)
    un = lax.broadcasted_iota(jnp.int32, (T_PAD, LANES), 1) - (LANES - t_new)
    lane_new = lax.broadcasted_iota(jnp.int32, (1, LANES), 1) >= LANES - t_new
    cache_masks, new_masks = [], []
    for window, dil in DILATED_BRANCHES:
        assert w % dil == 0 and dil & (dil - 1) == 0
        cache_masks.append((tq < t_new) & (tok >= w + tq - window)
                           & (((tok - tq) & (dil - 1)) == 0))
        new_masks.append((tqn < t_new) & (un >= 0) & (un <= tqn)
                         & (((tqn - un) & (dil - 1)) == 0))

    for h in range(n_heads):
        q = q_ref[0, h].astype(BF16)
        kt, vt = kt_ref[0, h], vt_ref[0, h]
        knp = knr_ref[row, h * dh:(h + 1) * dh, :]
        vnp = vnr_ref[row, h * dh:(h + 1) * dh, :]
        sc = _dot(q, kt.astype(BF16))
        sn = _dot(q, knp.astype(BF16))
        pcs, pns, dens, lses = [], [], [], []
        for cm, nm in zip(cache_masks, new_masks):
            scm = jnp.where(cm, sc, NEG)
            snm = jnp.where(nm, sn, NEG)
            m = jnp.maximum(jnp.max(scm, axis=-1, keepdims=True), jnp.max(snm, axis=-1, keepdims=True))
            pc, pn = jnp.exp(scm - m), jnp.exp(snm - m)
            den = jnp.sum(pc, axis=-1, keepdims=True) + jnp.sum(pn, axis=-1, keepdims=True)
            pcs.append(pc)
            pns.append(pn)
            dens.append(den)
            lses.append(m + jnp.log(den))
        o_all = (_dot_nt(jnp.concatenate(pcs, axis=0).astype(BF16), vt.astype(BF16))
                 + _dot_nt(jnp.concatenate(pns, axis=0).astype(BF16), vnp.astype(BF16)))
        mx = jnp.maximum(jnp.maximum(lses[0], lses[1]), lses[2])
        num, tot = 0.0, 0.0
        for bi in range(len(DILATED_BRANCHES)):
            wgt = jnp.exp(lses[bi] - mx)
            num = num + wgt * (o_all[bi * T_PAD:(bi + 1) * T_PAD] / dens[bi])
            tot = tot + wgt
        att_ref[0, h] = num / tot
        for src, newp, dst in ((kt, knp, ko_ref), (vt, vnp, vo_ref)):
            rolled = pltpu.roll(src, w - t_new, 1)
            dst[0, h, :, :w - LANES] = rolled[:, :w - LANES]
            dst[0, h, :, w - LANES:] = jnp.where(lane_new, newp, rolled[:, w - LANES:])


def _attn_sample(q4, knt, vnt, kt, vt, t_new):
    b, h, _, dh = q4.shape
    w = kt.shape[3]
    assert knt.shape == (1, h * dh, LANES) and b * t_new == LANES
    q_spec = pl.BlockSpec((1, h, T_PAD, dh), lambda i: (i, 0, 0, 0))
    new_spec = pl.BlockSpec(knt.shape, lambda i: (0, 0, 0))
    cache_spec = pl.BlockSpec((1, h, dh, w), lambda i: (i, 0, 0, 0))
    return pl.pallas_call(
        functools.partial(_attn_sample_kernel, t_new=t_new),
        grid=(b,),
        in_specs=[q_spec, new_spec, new_spec, cache_spec, cache_spec],
        out_specs=[q_spec, cache_spec, cache_spec],
        out_shape=[jax.ShapeDtypeStruct(q4.shape, F32), jax.ShapeDtypeStruct(kt.shape, F32),
                   jax.ShapeDtypeStruct(vt.shape, F32)],
        scratch_shapes=[pltpu.VMEM((b, h * dh, LANES), F32), pltpu.VMEM((b, h * dh, LANES), F32)],
        compiler_params=pltpu.CompilerParams(
            dimension_semantics=("arbitrary",), vmem_limit_bytes=VMEM_LIMIT_BYTES),
        name="attn_sample",
    )(q4, knt, vnt, kt, vt)


def _gla_decay_matrix(chunk):
    t = np.arange(chunk)[:, None]
    u = np.arange(chunk)[None, :]
    blocks = [u <= t, u > t]
    s = chunk // 2
    while s >= 1:
        mid = (t // (2 * s)) * (2 * s) + s
        upper = (t >= mid) & (u >= mid) & (u <= t)
        lower = (t < mid) & (u > t) & (u < mid)
        blocks.append(upper | lower)
        s //= 2
    return np.concatenate(blocks, axis=0).astype(np.float32)


def _gla_kernel(q_ref, k_ref, v_ref, g_ref, gb_ref, s0_ref, m_ref, gout_ref,
                o_ref, sfin_ref, st_ref, *, chunk):
    j = pl.program_id(1)
    n_lvl = int(math.log2(chunk))
    n_rows, tb = q_ref.shape[0], q_ref.shape[1]
    nc = tb // chunk
    hk = lax.broadcasted_iota(jnp.int32, (1, GK_W), 1) // DK_B
    hv = lax.broadcasted_iota(jnp.int32, (1, GV_W), 1) // DV_B
    hv_rows = lax.broadcasted_iota(jnp.int32, (GV_W, 1), 0) // DV_B
    state_mask = hv_rows == hk

    @pl.when(j == 0)
    def _():
        for r in range(n_rows):
            s0t = s0_ref[r].reshape(GK_W, DV_B).T
            st_ref[r] = jnp.where(state_mask, jnp.concatenate([s0t] * H_B, axis=0), 0.0)

    pi = lax.broadcasted_iota(jnp.int32, (chunk, H_B * chunk), 0)
    pj = lax.broadcasted_iota(jnp.int32, (chunk, H_B * chunk), 1) % chunk
    pair_masks = []
    s = chunk // 2
    while s >= 1:
        same = (pi // (2 * s)) == (pj // (2 * s))
        pair_masks.append(same & (pi % (2 * s) >= s) & (pj % (2 * s) < s))
        s //= 2
    pair_masks.append(pi == pj)
    a_head = lax.broadcasted_iota(jnp.int32, (H_B * chunk, 1), 0) // chunk
    bd_k = jnp.where(a_head == hk, 1.0, 0.0).astype(BF16)
    bd_v = jnp.where(a_head == hv, 1.0, 0.0).astype(BF16)

    mdec2 = m_ref[...]
    gout = gout_ref[...]
    units = [(r, slice(c * chunk, (c + 1) * chunk)) for r in range(n_rows) for c in range(nc)]
    qs = [q_ref[r, sl, :] for r, sl in units]
    ks = [k_ref[r, sl, :] for r, sl in units]
    xs = []
    for r, sl in units:
        g = g_ref[r, sl, :]
        g_hi = g.astype(BF16)
        g_lo = (g - g_hi.astype(F32)).astype(BF16)
        xs.append(jnp.exp(_dot(mdec2, jnp.concatenate([g_hi, g_lo], axis=0))))
    a_alls = [jnp.zeros((chunk, H_B * chunk), F32)] * len(units)
    for lvl in range(n_lvl + 1):
        for u in range(len(units)):
            if lvl < n_lvl:
                xl = xs[u][(2 + lvl) * chunk:(3 + lvl) * chunk]
                ql, kl = (qs[u] * xl).astype(BF16), (ks[u] * xl).astype(BF16)
            else:
                ql, kl = qs[u].astype(BF16), ks[u].astype(BF16)
            kl_bd = jnp.concatenate([kl] * H_B, axis=0) * bd_k
            a_alls[u] = jnp.where(pair_masks[lvl], _dot_nt(ql, kl_bd), a_alls[u])
    o_intra, q_dec, upds, decs = [], [], [], []
    for u, (r, sl) in enumerate(units):
        vb = v_ref[r, sl, :].astype(BF16)
        v_bd = jnp.concatenate([vb] * H_B, axis=0) * bd_v
        o_intra.append(_dot(a_alls[u].astype(BF16), v_bd))
        x_cum, x_rem = xs[u][0:chunk], xs[u][chunk:2 * chunk]
        q_dec.append((qs[u] * x_cum).astype(BF16))
        upds.append(jnp.where(state_mask, _dot_tn(vb, (ks[u] * x_rem).astype(BF16)), 0.0))
        decs.append(x_cum[chunk - 1:chunk, :])
    for r in range(n_rows):
        st = st_ref[r]
        for c in range(nc):
            u = r * nc + c
            sl = units[u][1]
            o = _dot_nt(q_dec[u], st.astype(BF16)) + o_intra[u]
            st = st * decs[u] + upds[u]
            gb = gb_ref[r, sl, :]
            for h in range(H_B):
                hs = slice(h * DV_B, (h + 1) * DV_B)
                gh = gb[:, hs]
                o_ref[r, sl, hs] = (_rms(o[:, hs], gout) * (gh * jax.nn.sigmoid(gh))).astype(o_ref.dtype)
        st_ref[r] = st

    @pl.when(j == pl.num_programs(1) - 1)
    def _():
        for r in range(n_rows):
            st = st_ref[r]
            s_t = st[0:DV_B]
            for h in range(1, H_B):
                s_t = s_t + st[h * DV_B:(h + 1) * DV_B]
            sfin_ref[r] = s_t.T.reshape(H_B, DK_B, DV_B)


def _gla(q, k, v, g, gb, s0, g_out, chunk, tb, rows):
    b, t, _ = q.shape
    m = _gla_decay_matrix(chunk)
    mdec = jnp.asarray(np.concatenate([m, m], axis=1), dtype=BF16)

    def tok(width):
        return pl.BlockSpec((rows, tb, width), lambda i, j: (i, j, 0))

    state_spec = pl.BlockSpec((rows, H_B, DK_B, DV_B), lambda i, j: (i, 0, 0, 0))
    return pl.pallas_call(
        functools.partial(_gla_kernel, chunk=chunk),
        grid=(b // rows, t // tb),
        in_specs=[tok(GK_W), tok(GK_W), tok(GV_W), tok(GK_W), tok(GV_W), state_spec,
                  pl.BlockSpec(mdec.shape, lambda i, j: (0, 0)),
                  pl.BlockSpec(g_out.shape, lambda i, j: (0, 0))],
        out_specs=[tok(GV_W), state_spec],
        out_shape=[jax.ShapeDtypeStruct((b, t, GV_W), BF16),
                   jax.ShapeDtypeStruct((b, H_B, DK_B, DV_B), F32)],
        scratch_shapes=[pltpu.VMEM((rows, GV_W, GK_W), F32)],
        compiler_params=pltpu.CompilerParams(
            dimension_semantics=("parallel", "arbitrary"), vmem_limit_bytes=VMEM_LIMIT_BYTES),
        name="gla",
    )(q, k, v, g, gb, s0, mdec, g_out)


def _rope_tables(pos):
    half = ROT_DIM // 2
    inv_freq = ROPE_THETA ** (-jnp.arange(half, dtype=F32) * (2.0 / ROT_DIM))
    ang = pos.astype(F32)[:, None] * inv_freq[None, :]
    cos, sin = jnp.cos(ang), jnp.sin(ang)
    n = pos.shape[0]
    ones = jnp.ones((n, HD_A - ROT_DIM), F32)
    zeros = jnp.zeros((n, HD_A - ROT_DIM), F32)
    zh = jnp.zeros((n, half), F32)
    c = jnp.concatenate([cos, cos, ones], axis=1)
    sa = jnp.concatenate([zh, sin, zeros], axis=1)
    sb = jnp.concatenate([-sin, zh, zeros], axis=1)
    rep = LANES // HD_A
    return tuple(jnp.tile(tab, (1, rep)) for tab in (c, sa, sb))


def _prep_weights(g_ffn1, w_ffn1_gu, w_ffn1_down, g_mix, w_in, w_gla_a2, b_gla_a, g_gla_out, w_out,
                  g_ffn2, w_ffn2_gu, w_ffn2_down, g_ple, w_ple_gate, w_ple_proj, g_final):
    win = w_in[0].T
    wa2 = w_gla_a2[0]
    return {
        "g_ffn1": g_ffn1, "w_ffn1_gu": w_ffn1_gu[0].astype(BF16),
        "w_ffn1_down": w_ffn1_down[0].astype(BF16), "g_mix": g_mix,
        "w_in": win.astype(BF16), "w_gla_a2": wa2.astype(BF16), "b_gla_a": b_gla_a,
        "g_gla_out": g_gla_out, "w_out": w_out[0].astype(BF16), "g_ffn2": g_ffn2,
        "w_ffn2_gu": w_ffn2_gu[0].astype(BF16), "w_ffn2_down": w_ffn2_down[0].astype(BF16),
        "g_ple": g_ple, "w_ple_gate": w_ple_gate[0].astype(BF16),
        "w_ple_proj": w_ple_proj[0].astype(BF16), "g_final": g_final.reshape(1, -1),
    }


def kernel(x_prompt, x_sample, cache_k_win, cache_v_win, state_gla, p_prompt, p_sample, g_ffn1, w_ffn1_gu, w_ffn1_down, g_mix, w_in, w_gla_a2, b_gla_a, g_gla_out, w_out, g_ffn2, w_ffn2_gu, w_ffn2_down, g_ple, w_ple_gate, w_ple_proj, g_final):
    assert w_in.shape[0] == 1, "single-layer step"
    w = _prep_weights(g_ffn1, w_ffn1_gu, w_ffn1_down, g_mix, w_in, w_gla_a2, b_gla_a, g_gla_out,
                      w_out, g_ffn2, w_ffn2_gu, w_ffn2_down, g_ple, w_ple_gate, w_ple_proj, g_final)
    bp, seq, d = x_prompt.shape
    bs, t_new, _ = x_sample.shape
    win_buf = cache_k_win.shape[2]
    assert seq == DILATED_BRANCHES[-1][0] == win_buf

    tm = 512
    rope_p = _rope_tables(jnp.arange(seq, dtype=jnp.int32))
    (h1, qa, ka, va, qb, kb, vb, gk, gb, kat, vat) = _pre_mixer(
        x_prompt.reshape(bp * seq, d), w, rope_p, tm, seq // tm)
    r3 = lambda a: a.reshape(bp, seq, a.shape[-1])
    att_p = _attn_prompt(r3(qa), r3(ka), r3(va))
    gla_p, state_p = _gla(r3(qb), r3(kb), r3(vb), r3(gk), r3(gb),
                          jnp.zeros((bp, H_B, DK_B, DV_B), F32), w["g_gla_out"],
                          chunk=GLA_CHUNK, tb=8 * GLA_CHUNK, rows=1)
    y_prompt = _post_mixer(h1, att_p.reshape(bp * seq, ATT_W), gla_p.reshape(bp * seq, GV_W),
                           p_prompt[0].reshape(bp * seq, -1), w, tm).reshape(bp, seq, d)
    win5 = lambda a: a.reshape(bp, H_A, HD_A, seq).transpose(0, 3, 1, 2)[None]
    k_win_prompt, v_win_prompt = win5(kat), win5(vat)

    ns = bs * t_new
    rope_s = _rope_tables(jnp.tile(PAST_LEN + jnp.arange(t_new, dtype=jnp.int32), bs))
    (h1s, qas, _, _, qbs, kbs, vbs, gks, gbs, kats, vats) = _pre_mixer(
        x_sample.reshape(ns, d), w, rope_s, ns, 1)
    q4 = jnp.pad(qas.reshape(bs, t_new, H_A, HD_A).transpose(0, 2, 1, 3),
                 ((0, 0), (0, 0), (0, T_PAD - t_new), (0, 0)))
    tok_last = lambda c: c[0].transpose(0, 2, 3, 1)
    att4, kto, vto = _attn_sample(q4, kats, vats, tok_last(cache_k_win), tok_last(cache_v_win),
                                  t_new)
    att_s = att4[:, :, :t_new].transpose(0, 2, 1, 3)
    k_win_sample = kto.transpose(0, 3, 1, 2)
    v_win_sample = vto.transpose(0, 3, 1, 2)
    s_chunk = 16
    pad = lambda a: jnp.pad(a.reshape(bs, t_new, a.shape[-1]), ((0, 0), (0, s_chunk - t_new), (0, 0)))
    gla_s, state_s = _gla(pad(qbs), pad(kbs), pad(vbs), pad(gks), pad(gbs), state_gla[0],
                          w["g_gla_out"], chunk=s_chunk, tb=s_chunk, rows=math.gcd(bs, 8))
    y_sample = _post_mixer(h1s, att_s.reshape(ns, ATT_W).astype(BF16),
                           gla_s[:, :t_new].reshape(ns, GV_W), p_sample[0].reshape(ns, -1),
                           w, ns).reshape(bs, t_new, d)

    return (y_prompt, y_sample, k_win_prompt, v_win_prompt, state_p[None],
            k_win_sample[None], v_win_sample[None], state_s[None])
```

```python
import functools
import math

import jax
import jax.numpy as jnp
import numpy as np
from jax import lax
from jax.experimental import pallas as pl
from jax.experimental.pallas import tpu as pltpu

F32 = jnp.float32
BF16 = jnp.bfloat16

H_A = 8
HD_A = 64
ATT_W = H_A * HD_A
DILATED_BRANCHES = ((128, 1), (512, 4), (2048, 16))
ROT_DIM = HD_A // 4
ROPE_THETA = 500000.0
PAST_LEN = 8192
H_B = 4
DK_B = 64
DV_B = 128
GK_W = H_B * DK_B
GV_W = H_B * DV_B
GATE_RANK = 16
GATE_NORM = 16.0
GLA_CHUNK = 64
EPS = 1e-6
NEG = -1e30

LANES = 128
MXU_TILE = 256
FFN_CHUNKS = 4
ROW_GROUP = 256
VMEM_LIMIT_BYTES = 56 * 1024 * 1024
ATT_BLK = 128
ATT_UNROLL = 16


def _rms(x, g):
    return x * lax.rsqrt(jnp.mean(x * x, axis=-1, keepdims=True) + EPS) * g


def _dot(a, b):
    return jnp.dot(a, b, preferred_element_type=F32)


def _dot_nt(a, b):
    return lax.dot_general(a, b, (((1,), (1,)), ((), ())), preferred_element_type=F32)


def _dot_tn(a, b):
    return lax.dot_general(a, b, (((0,), (0,)), ((), ())), preferred_element_type=F32)


def _row_groups(ref_rows):
    n = ref_rows // ROW_GROUP if ref_rows % ROW_GROUP == 0 else 1
    step = ref_rows // n
    return [slice(i * step, (i + 1) * step) for i in range(n)]


def _swiglu_half_step(xs, g_ref, wgu_ref, wdn_ref):
    d_ff = wdn_ref.shape[0]
    xns = [_rms(x, g_ref[...]).astype(BF16) for x in xs]
    n_tiles = -(-d_ff // MXU_TILE)
    n_chunk = min(FFN_CHUNKS, n_tiles)
    edges = [min(d_ff, (n_tiles * c // n_chunk) * MXU_TILE) for c in range(n_chunk + 1)]
    accs = [None] * len(xs)
    for lo, hi in zip(edges[:-1], edges[1:]):
        for i, xn in enumerate(xns):
            gate = _dot(xn, wgu_ref[:, lo:hi])
            up = _dot(xn, wgu_ref[:, d_ff + lo:d_ff + hi])
            act = (gate * jax.nn.sigmoid(gate) * up).astype(BF16)
            part = _dot(act, wdn_ref[lo:hi, :])
            accs[i] = part if accs[i] is None else accs[i] + part
    return [x + 0.5 * acc for x, acc in zip(xs, accs)]


def _pre_mixer_kernel(x_ref, g1_ref, wgu_ref, wdn_ref, gm_ref, win_ref, wa2_ref, ba_ref,
                      rc_ref, rsa_ref, rsb_ref,
                      h1_ref, qa_ref, ka_ref, va_ref, qb_ref, kb_ref, vb_ref, gk_ref, gb_ref,
                      kat_ref, vat_ref):
    groups = _row_groups(x_ref.shape[0])
    h1s = _swiglu_half_step([x_ref[sl, :] for sl in groups], g1_ref, wgu_ref, wdn_ref)
    for sl, h1 in zip(groups, h1s):
        h1_ref[sl, :] = h1
    xms = [_rms(h1, gm_ref[...]).astype(BF16) for h1 in h1s]

    def rope(t, sl, scale):
        rc, rsa, rsb = rc_ref[sl, :], rsa_ref[sl, :], rsb_ref[sl, :]
        cols = []
        for c in range(ATT_W // LANES):
            blk = t[:, c * LANES:(c + 1) * LANES]
            rot = (blk * rc + pltpu.roll(blk, ROT_DIM // 2, 1) * rsa
                   + pltpu.roll(blk, LANES - ROT_DIM // 2, 1) * rsb)
            cols.append(rot * scale if scale != 1.0 else rot)
        return jnp.concatenate(cols, axis=1)

    off = 3 * ATT_W
    for sl, xm in zip(groups, xms):
        def proj(lo, width, xm=xm):
            return _dot_nt(xm, win_ref[lo:lo + width, :])

        qa_ref[sl, :] = rope(proj(0, ATT_W), sl, HD_A ** -0.5)
        ka = rope(proj(ATT_W, ATT_W), sl, 1.0)
        va = proj(2 * ATT_W, ATT_W)
        ka_ref[sl, :] = ka
        va_ref[sl, :] = va
        kat_ref[0, :, sl] = ka.T
        vat_ref[0, :, sl] = va.T
        qb_ref[sl, :] = proj(off, GK_W) * (DK_B ** -0.5)
        kb_ref[sl, :] = proj(off + GK_W, GK_W)
        vb_ref[sl, :] = proj(off + 2 * GK_W, GV_W)
        a1 = proj(off + 2 * GK_W + GV_W, GATE_RANK)
        gb_ref[sl, :] = proj(off + 2 * GK_W + GV_W + GATE_RANK, GV_W)
        z = _dot(a1.astype(BF16), wa2_ref[...]) + ba_ref[...]
        gk_ref[sl, :] = (jnp.minimum(z, 0.0) - jnp.log1p(jnp.exp(-jnp.abs(z)))) * (1.0 / GATE_NORM)


def _pre_mixer(x, w, rope_tabs, tm, rope_period_blocks):
    n, d = x.shape
    grid = (n // tm,)

    def tok(width):
        return pl.BlockSpec((tm, width), lambda i: (i, 0))

    def full(a):
        return pl.BlockSpec(a.shape, lambda i: (0,) * a.ndim)

    rope_spec = pl.BlockSpec((tm, LANES), lambda i: (i % rope_period_blocks, 0))
    weights = (w["g_ffn1"], w["w_ffn1_gu"], w["w_ffn1_down"], w["g_mix"], w["w_in"],
               w["w_gla_a2"], w["b_gla_a"])
    out_widths = (d, ATT_W, ATT_W, ATT_W, GK_W, GK_W, GV_W, GK_W, GV_W)
    tps = tm * rope_period_blocks
    tok_last = pl.BlockSpec((1, ATT_W, tm),
                            lambda i: (i // rope_period_blocks, 0, i % rope_period_blocks))
    tok_last_shape = jax.ShapeDtypeStruct((n // tps, ATT_W, tps), F32)
    return pl.pallas_call(
        _pre_mixer_kernel,
        grid=grid,
        in_specs=[tok(d)] + [full(a) for a in weights] + [rope_spec] * 3,
        out_specs=[tok(wd) for wd in out_widths] + [tok_last] * 2,
        out_shape=[jax.ShapeDtypeStruct((n, wd), F32) for wd in out_widths] + [tok_last_shape] * 2,
        compiler_params=pltpu.CompilerParams(
            dimension_semantics=("parallel",), vmem_limit_bytes=VMEM_LIMIT_BYTES),
        name="pre_mixer",
    )(x, *weights, *rope_tabs)


def _post_mixer_kernel(h1_ref, att_ref, gla_ref, p_ref, wout_ref, g2_ref, wgu_ref, wdn_ref,
                       gp_ref, wpg_ref, wpp_ref, gf_ref, y_ref):
    groups = _row_groups(h1_ref.shape[0])
    h2s = [h1_ref[sl, :] + _dot(att_ref[sl, :], wout_ref[:ATT_W, :])
           + _dot(gla_ref[sl, :], wout_ref[ATT_W:, :]) for sl in groups]
    h3s = _swiglu_half_step(h2s, g2_ref, wgu_ref, wdn_ref)
    for sl, h3 in zip(groups, h3s):
        gate = jax.nn.sigmoid(_dot(_rms(h3, gp_ref[...]).astype(BF16), wpg_ref[...]))
        h4 = h3 + gate * _dot(p_ref[sl, :].astype(BF16), wpp_ref[...])
        y_ref[sl, :] = _rms(h4, gf_ref[...])


def _post_mixer(h1, att, gla, p, w, tm):
    n, d = h1.shape
    grid = (n // tm,)

    def tok(width):
        return pl.BlockSpec((tm, width), lambda i: (i, 0))

    def full(a):
        return pl.BlockSpec(a.shape, lambda i: (0,) * a.ndim)

    weights = (w["w_out"], w["g_ffn2"], w["w_ffn2_gu"], w["w_ffn2_down"], w["g_ple"],
               w["w_ple_gate"], w["w_ple_proj"], w["g_final"])
    return pl.pallas_call(
        _post_mixer_kernel,
        grid=grid,
        in_specs=[tok(d), tok(ATT_W), tok(GV_W), tok(p.shape[1])] + [full(a) for a in weights],
        out_specs=tok(d),
        out_shape=jax.ShapeDtypeStruct((n, d), F32),
        compiler_params=pltpu.CompilerParams(
            dimension_semantics=("parallel",), vmem_limit_bytes=VMEM_LIMIT_BYTES),
        name="post_mixer",
    )(h1, att, gla, p, *weights)


def _attn_prompt_kernel(q_ref, k_ref, v_ref, o_ref, oacc_ref, max_ref, den_ref):
    seq = q_ref.shape[1]
    qv, kv, vv = q_ref.at[0], k_ref.at[0], v_ref.at[0]
    blk = ATT_BLK
    lane = lax.broadcasted_iota(jnp.int32, (1, LANES), 1)
    head0 = lane < HD_A
    qi = lax.broadcasted_iota(jnp.int32, (blk, 2 * blk), 0)
    kj = lax.broadcasted_iota(jnp.int32, (blk, 2 * blk), 1)
    in_cur = (kj >= blk) & (kj - blk <= qi)
    in_prev = (kj < blk) & (qi <= kj)
    bias_first = jnp.where(in_cur, 0.0, NEG)
    bias_rest = jnp.where(in_cur | in_prev, 0.0, NEG)

    for bi, (window, dil) in enumerate(DILATED_BRANCHES):
        assert window // dil == blk
        nb = seq // (dil * blk)

        def unit(idx, dil=dil, nb=nb, bi=bi):
            r = idx // nb
            n = idx % nb
            start = r + n * (blk * dil)
            pstart = jnp.maximum(start - blk * dil, r)
            if dil == 1:
                start = pl.multiple_of(start, blk)
                pstart = pl.multiple_of(pstart, blk)

            def rows(ref, s0):
                if dil == 1:
                    return ref[pl.ds(s0, blk), :]
                return ref[pl.ds(s0, blk, stride=dil), :]

            qb = rows(qv, start)
            if nb == 1:
                none = jnp.zeros((blk, LANES), F32)
                kcat = jnp.concatenate([none, rows(kv, start)], axis=0).astype(BF16)
                vcat = jnp.concatenate([none, rows(vv, start)], axis=0).astype(BF16)
                bias = bias_first
            else:
                kcat = jnp.concatenate([rows(kv, pstart), rows(kv, start)], axis=0).astype(BF16)
                vcat = jnp.concatenate([rows(vv, pstart), rows(vv, start)], axis=0).astype(BF16)
                bias = jnp.where(n > 0, bias_rest, bias_first)
            outs, maxes, dens = [], [], []
            for hmask in (head0, ~head0):
                qh = jnp.where(hmask, qb, 0.0).astype(BF16)
                s = _dot_nt(qh, kcat) + bias
                m = jnp.max(s, axis=-1, keepdims=True)
                p = jnp.exp(s - m)
                maxes.append(m)
                dens.append(jnp.sum(p, axis=-1, keepdims=True))
                outs.append(_dot(p.astype(BF16), vcat))
            stats = (jnp.where(head0, outs[0], outs[1]), jnp.where(head0, maxes[0], maxes[1]),
                     jnp.where(head0, dens[0], dens[1]))
            for ref, val in zip((oacc_ref, max_ref, den_ref), stats):
                if dil == 1:
                    ref[bi, pl.ds(start, blk), :] = val
                else:
                    ref.at[bi][pl.ds(start, blk, stride=dil), :] = val

        def body(it, carry, unit=unit):
            for u in range(ATT_UNROLL):
                unit(it * ATT_UNROLL + u)
            return carry

        lax.fori_loop(0, seq // (blk * ATT_UNROLL), body, 0)

    rows_per = 256
    nbr = len(DILATED_BRANCHES)
    for c in range(seq // rows_per):
        sl = slice(c * rows_per, (c + 1) * rows_per)
        ms = [max_ref[b, sl, :] for b in range(nbr)]
        top = functools.reduce(jnp.maximum, ms)
        es = [jnp.exp(m - top) for m in ms]
        num = sum(e * oacc_ref[b, sl, :] for b, e in enumerate(es))
        tot = sum(e * den_ref[b, sl, :] for b, e in enumerate(es))
        o_ref[0, sl, :] = (num / tot).astype(o_ref.dtype)


def _attn_prompt(q, k, v):
    b, seq, width = q.shape
    spec = pl.BlockSpec((1, seq, LANES), lambda i, j: (i, 0, j))
    nbr = len(DILATED_BRANCHES)
    return pl.pallas_call(
        _attn_prompt_kernel,
        grid=(b, width // LANES),
        in_specs=[spec] * 3,
        out_specs=spec,
        out_shape=jax.ShapeDtypeStruct((b, seq, width), BF16),
        scratch_shapes=[pltpu.VMEM((nbr, seq, LANES), F32)] * 3,
        compiler_params=pltpu.CompilerParams(
            dimension_semantics=("parallel", "parallel"), vmem_limit_bytes=VMEM_LIMIT_BYTES),
        name="attn_prompt",
    )(q, k, v)


T_PAD = 8


def _attn_sample_kernel(q_ref, knt_ref, vnt_ref, kt_ref, vt_ref, att_ref, ko_ref, vo_ref,
                        knr_ref, vnr_ref, *, t_new):
    n_heads, dh, w = kt_ref.shape[1], kt_ref.shape[2], kt_ref.shape[3]
    row = pl.program_id(0)

    @pl.when(row == 0)
    def _():
        for b in range(knr_ref.shape[0]):
            shift = LANES - t_new - b * t_new
            knr_ref[b] = pltpu.roll(knt_ref[0], shift, 1)
            vnr_ref[b] = pltpu.roll(vnt_ref[0], shift, 1)
    tq = lax.broadcasted_iota(jnp.int32, (T_PAD, w), 0)
    tok = lax.broadcasted_iota(jnp.int32, (T_PAD, w), 1)
    tqn = lax.broadcasted_iota(jnp.int32, (T_PAD, LANES), 0)
    un = lax.broadcasted_iota(jnp.int32, (T_PAD, LANES), 1) - (LANES - t_new)
    lane_new = lax.broadcasted_iota(jnp.int32, (1, LANES), 1) >= LANES - t_new
    cache_masks, new_masks = [], []
    for window, dil in DILATED_BRANCHES:
        assert w % dil == 0 and dil & (dil - 1) == 0
        cache_masks.append((tq < t_new) & (tok >= w + tq - window)
                           & (((tok - tq) & (dil - 1)) == 0))
        new_masks.append((tqn < t_new) & (un >= 0) & (un <= tqn)
                         & (((tqn - un) & (dil - 1)) == 0))

    for h in range(n_heads):
        q = q_ref[0, h].astype(BF16)
        kt, vt = kt_ref[0, h], vt_ref[0, h]
        knp = knr_ref[row, h * dh:(h + 1) * dh, :]
        vnp = vnr_ref[row, h * dh:(h + 1) * dh, :]
        sc = _dot(q, kt.astype(BF16))
        sn = _dot(q, knp.astype(BF16))
        pcs, pns, dens, lses = [], [], [], []
        for cm, nm in zip(cache_masks, new_masks):
            scm = jnp.where(cm, sc, NEG)
            snm = jnp.where(nm, sn, NEG)
            m = jnp.maximum(jnp.max(scm, axis=-1, keepdims=True), jnp.max(snm, axis=-1, keepdims=True))
            pc, pn = jnp.exp(scm - m), jnp.exp(snm - m)
            den = jnp.sum(pc, axis=-1, keepdims=True) + jnp.sum(pn, axis=-1, keepdims=True)
            pcs.append(pc)
            pns.append(pn)
            dens.append(den)
            lses.append(m + jnp.log(den))
        o_all = (_dot_nt(jnp.concatenate(pcs, axis=0).astype(BF16), vt.astype(BF16))
                 + _dot_nt(jnp.concatenate(pns, axis=0).astype(BF16), vnp.astype(BF16)))
        mx = jnp.maximum(jnp.maximum(lses[0], lses[1]), lses[2])
        num, tot = 0.0, 0.0
        for bi in range(len(DILATED_BRANCHES)):
            wgt = jnp.exp(lses[bi] - mx)
            num = num + wgt * (o_all[bi * T_PAD:(bi + 1) * T_PAD] / dens[bi])
            tot = tot + wgt
        att_ref[0, h] = num / tot
        for src, newp, dst in ((kt, knp, ko_ref), (vt, vnp, vo_ref)):
            rolled = pltpu.roll(src, w - t_new, 1)
            dst[0, h, :, :w - LANES] = rolled[:, :w - LANES]
            dst[0, h, :, w - LANES:] = jnp.where(lane_new, newp, rolled[:, w - LANES:])


def _attn_sample(q4, knt, vnt, kt, vt, t_new):
    b, h, _, dh = q4.shape
    w = kt.shape[3]
    assert knt.shape == (1, h * dh, LANES) and b * t_new == LANES
    q_spec = pl.BlockSpec((1, h, T_PAD, dh), lambda i: (i, 0, 0, 0))
    new_spec = pl.BlockSpec(knt.shape, lambda i: (0, 0, 0))
    cache_spec = pl.BlockSpec((1, h, dh, w), lambda i: (i, 0, 0, 0))
    return pl.pallas_call(
        functools.partial(_attn_sample_kernel, t_new=t_new),
        grid=(b,),
        in_specs=[q_spec, new_spec, new_spec, cache_spec, cache_spec],
        out_specs=[q_spec, cache_spec, cache_spec],
        out_shape=[jax.ShapeDtypeStruct(q4.shape, F32), jax.ShapeDtypeStruct(kt.shape, F32),
                   jax.ShapeDtypeStruct(vt.shape, F32)],
        scratch_shapes=[pltpu.VMEM((b, h * dh, LANES), F32), pltpu.VMEM((b, h * dh, LANES), F32)],
        compiler_params=pltpu.CompilerParams(
            dimension_semantics=("arbitrary",), vmem_limit_bytes=VMEM_LIMIT_BYTES),
        name="attn_sample",
    )(q4, knt, vnt, kt, vt)


def _gla_decay_matrix(chunk):
    t = np.arange(chunk)[:, None]
    u = np.arange(chunk)[None, :]
    blocks = [u <= t, u > t]
    s = chunk // 2
    while s >= 1:
        mid = (t // (2 * s)) * (2 * s) + s
        upper = (t >= mid) & (u >= mid) & (u <= t)
        lower = (t < mid) & (u > t) & (u < mid)
        blocks.append(upper | lower)
        s //= 2
    return np.concatenate(blocks, axis=0).astype(np.float32)


def _gla_kernel(q_ref, k_ref, v_ref, g_ref, gb_ref, s0_ref, m_ref, gout_ref,
                o_ref, sfin_ref, st_ref, *, chunk):
    j = pl.program_id(1)
    n_lvl = int(math.log2(chunk))
    n_rows, tb = q_ref.shape[0], q_ref.shape[1]
    nc = tb // chunk
    hk = lax.broadcasted_iota(jnp.int32, (1, GK_W), 1) // DK_B
    hv = lax.broadcasted_iota(jnp.int32, (1, GV_W), 1) // DV_B
    hv_rows = lax.broadcasted_iota(jnp.int32, (GV_W, 1), 0) // DV_B
    state_mask = hv_rows == hk

    @pl.when(j == 0)
    def _():
        for r in range(n_rows):
            s0t = s0_ref[r].reshape(GK_W, DV_B).T
            st_ref[r] = jnp.where(state_mask, jnp.concatenate([s0t] * H_B, axis=0), 0.0)

    pi = lax.broadcasted_iota(jnp.int32, (chunk, H_B * chunk), 0)
    pj = lax.broadcasted_iota(jnp.int32, (chunk, H_B * chunk), 1) % chunk
    pair_masks = []
    s = chunk // 2
    while s >= 1:
        same = (pi // (2 * s)) == (pj // (2 * s))
        pair_masks.append(same & (pi % (2 * s) >= s) & (pj % (2 * s) < s))
        s //= 2
    pair_masks.append(pi == pj)
    a_head = lax.broadcasted_iota(jnp.int32, (H_B * chunk, 1), 0) // chunk
    bd_k = jnp.where(a_head == hk, 1.0, 0.0).astype(BF16)
    bd_v = jnp.where(a_head == hv, 1.0, 0.0).astype(BF16)

    mdec2 = m_ref[...]
    gout = gout_ref[...]
    units = [(r, slice(c * chunk, (c + 1) * chunk)) for r in range(n_rows) for c in range(nc)]
    qs = [q_ref[r, sl, :] for r, sl in units]
    ks = [k_ref[r, sl, :] for r, sl in units]
    xs = []
    for r, sl in units:
        g = g_ref[r, sl, :]
        g_hi = g.astype(BF16)
        g_lo = (g - g_hi.astype(F32)).astype(BF16)
        xs.append(jnp.exp(_dot(mdec2, jnp.concatenate([g_hi, g_lo], axis=0))))
    a_alls = [jnp.zeros((chunk, H_B * chunk), F32)] * len(units)
    for lvl in range(n_lvl + 1):
        for u in range(len(units)):
            if lvl < n_lvl:
                xl = xs[u][(2 + lvl) * chunk:(3 + lvl) * chunk]
                ql, kl = (qs[u] * xl).astype(BF16), (ks[u] * xl).astype(BF16)
            else:
                ql, kl = qs[u].astype(BF16), ks[u].astype(BF16)
            kl_bd = jnp.concatenate([kl] * H_B, axis=0) * bd_k
            a_alls[u] = jnp.where(pair_masks[lvl], _dot_nt(ql, kl_bd), a_alls[u])
    o_intra, q_dec, upds, decs = [], [], [], []
    for u, (r, sl) in enumerate(units):
        vb = v_ref[r, sl, :].astype(BF16)
        v_bd = jnp.concatenate([vb] * H_B, axis=0) * bd_v
        o_intra.append(_dot(a_alls[u].astype(BF16), v_bd))
        x_cum, x_rem = xs[u][0:chunk], xs[u][chunk:2 * chunk]
        q_dec.append((qs[u] * x_cum).astype(BF16))
        upds.append(jnp.where(state_mask, _dot_tn(vb, (ks[u] * x_rem).astype(BF16)), 0.0))
        decs.append(x_cum[chunk - 1:chunk, :])
    for r in range(n_rows):
        st = st_ref[r]
        for c in range(nc):
            u = r * nc + c
            sl = units[u][1]
            o = _dot_nt(q_dec[u], st.astype(BF16)) + o_intra[u]
            st = st * decs[u] + upds[u]
            gb = gb_ref[r, sl, :]
            for h in range(H_B):
                hs = slice(h * DV_B, (h + 1) * DV_B)
                gh = gb[:, hs]
                o_ref[r, sl, hs] = (_rms(o[:, hs], gout) * (gh * jax.nn.sigmoid(gh))).astype(o_ref.dtype)
        st_ref[r] = st

    @pl.when(j == pl.num_programs(1) - 1)
    def _():
        for r in range(n_rows):
            st = st_ref[r]
            s_t = st[0:DV_B]
            for h in range(1, H_B):
                s_t = s_t + st[h * DV_B:(h + 1) * DV_B]
            sfin_ref[r] = s_t.T.reshape(H_B, DK_B, DV_B)


def _gla(q, k, v, g, gb, s0, g_out, chunk, tb, rows):
    b, t, _ = q.shape
    m = _gla_decay_matrix(chunk)
    mdec = jnp.asarray(np.concatenate([m, m], axis=1), dtype=BF16)

    def tok(width):
        return pl.BlockSpec((rows, tb, width), lambda i, j: (i, j, 0))

    state_spec = pl.BlockSpec((rows, H_B, DK_B, DV_B), lambda i, j: (i, 0, 0, 0))
    return pl.pallas_call(
        functools.partial(_gla_kernel, chunk=chunk),
        grid=(b // rows, t // tb),
        in_specs=[tok(GK_W), tok(GK_W), tok(GV_W), tok(GK_W), tok(GV_W), state_spec,
                  pl.BlockSpec(mdec.shape, lambda i, j: (0, 0)),
                  pl.BlockSpec(g_out.shape, lambda i, j: (0, 0))],
        out_specs=[tok(GV_W), state_spec],
        out_shape=[jax.ShapeDtypeStruct((b, t, GV_W), BF16),
                   jax.ShapeDtypeStruct((b, H_B, DK_B, DV_B), F32)],
        scratch_shapes=[pltpu.VMEM((rows, GV_W, GK_W), F32)],
        compiler_params=pltpu.CompilerParams(
            dimension_semantics=("parallel", "arbitrary"), vmem_limit_bytes=VMEM_LIMIT_BYTES),
        name="gla",
    )(q, k, v, g, gb, s0, mdec, g_out)


def _rope_tables(pos):
    half = ROT_DIM // 2
    inv_freq = ROPE_THETA ** (-jnp.arange(half, dtype=F32) * (2.0 / ROT_DIM))
    ang = pos.astype(F32)[:, None] * inv_freq[None, :]
    cos, sin = jnp.cos(ang), jnp.sin(ang)
    n = pos.shape[0]
    ones = jnp.ones((n, HD_A - ROT_DIM), F32)
    zeros = jnp.zeros((n, HD_A - ROT_DIM), F32)
    zh = jnp.zeros((n, half), F32)
    c = jnp.concatenate([cos, cos, ones], axis=1)
    sa = jnp.concatenate([zh, sin, zeros], axis=1)
    sb = jnp.concatenate([-sin, zh, zeros], axis=1)
    rep = LANES // HD_A
    return tuple(jnp.tile(tab, (1, rep)) for tab in (c, sa, sb))


def _prep_weights(g_ffn1, w_ffn1_gu, w_ffn1_down, g_mix, w_in, w_gla_a2, b_gla_a, g_gla_out, w_out,
                  g_ffn2, w_ffn2_gu, w_ffn2_down, g_ple, w_ple_gate, w_ple_proj, g_final):
    win = w_in[0].T
    wa2 = w_gla_a2[0]
    return {
        "g_ffn1": g_ffn1, "w_ffn1_gu": w_ffn1_gu[0].astype(BF16),
        "w_ffn1_down": w_ffn1_down[0].astype(BF16), "g_mix": g_mix,
        "w_in": win.astype(BF16), "w_gla_a2": wa2.astype(BF16), "b_gla_a": b_gla_a,
        "g_gla_out": g_gla_out, "w_out": w_out[0].astype(BF16), "g_ffn2": g_ffn2,
        "w_ffn2_gu": w_ffn2_gu[0].astype(BF16), "w_ffn2_down": w_ffn2_down[0].astype(BF16),
        "g_ple": g_ple, "w_ple_gate": w_ple_gate[0].astype(BF16),
        "w_ple_proj": w_ple_proj[0].astype(BF16), "g_final": g_final.reshape(1, -1),
    }


def kernel(x_prompt, x_sample, cache_k_win, cache_v_win, state_gla, p_prompt, p_sample, g_ffn1, w_ffn1_gu, w_ffn1_down, g_mix, w_in, w_gla_a2, b_gla_a, g_gla_out, w_out, g_ffn2, w_ffn2_gu, w_ffn2_down, g_ple, w_ple_gate, w_ple_proj, g_final):
    assert w_in.shape[0] == 1, "single-layer step"
    w = _prep_weights(g_ffn1, w_ffn1_gu, w_ffn1_down, g_mix, w_in, w_gla_a2, b_gla_a, g_gla_out,
                      w_out, g_ffn2, w_ffn2_gu, w_ffn2_down, g_ple, w_ple_gate, w_ple_proj, g_final)
    bp, seq, d = x_prompt.shape
    bs, t_new, _ = x_sample.shape
    win_buf = cache_k_win.shape[2]
    assert seq == DILATED_BRANCHES[-1][0] == win_buf

    tm = 512
    rope_p = _rope_tables(jnp.arange(seq, dtype=jnp.int32))
    (h1, qa, ka, va, qb, kb, vb, gk, gb, kat, vat) = _pre_mixer(
        x_prompt.reshape(bp * seq, d), w, rope_p, tm, seq // tm)
    r3 = lambda a: a.reshape(bp, seq, a.shape[-1])
    att_p = _attn_prompt(r3(qa), r3(ka), r3(va))
    gla_p, state_p = _gla(r3(qb), r3(kb), r3(vb), r3(gk), r3(gb),
                          jnp.zeros((bp, H_B, DK_B, DV_B), F32), w["g_gla_out"],
                          chunk=GLA_CHUNK, tb=16 * GLA_CHUNK, rows=1)
    y_prompt = _post_mixer(h1, att_p.reshape(bp * seq, ATT_W), gla_p.reshape(bp * seq, GV_W),
                           p_prompt[0].reshape(bp * seq, -1), w, 2 * tm).reshape(bp, seq, d)
    win5 = lambda a: a.reshape(bp, H_A, HD_A, seq).transpose(0, 3, 1, 2)[None]
    k_win_prompt, v_win_prompt = win5(kat), win5(vat)

    ns = bs * t_new
    rope_s = _rope_tables(jnp.tile(PAST_LEN + jnp.arange(t_new, dtype=jnp.int32), bs))
    (h1s, qas, _, _, qbs, kbs, vbs, gks, gbs, kats, vats) = _pre_mixer(
        x_sample.reshape(ns, d), w, rope_s, ns, 1)
    q4 = jnp.pad(qas.reshape(bs, t_new, H_A, HD_A).transpose(0, 2, 1, 3),
                 ((0, 0), (0, 0), (0, T_PAD - t_new), (0, 0)))
    tok_last = lambda c: c[0].transpose(0, 2, 3, 1)
    att4, kto, vto = _attn_sample(q4, kats, vats, tok_last(cache_k_win), tok_last(cache_v_win),
                                  t_new)
    att_s = att4[:, :, :t_new].transpose(0, 2, 1, 3)
    k_win_sample = kto.transpose(0, 3, 1, 2)
    v_win_sample = vto.transpose(0, 3, 1, 2)
    s_chunk = 16
    pad = lambda a: jnp.pad(a.reshape(bs, t_new, a.shape[-1]), ((0, 0), (0, s_chunk - t_new), (0, 0)))
    gla_s, state_s = _gla(pad(qbs), pad(kbs), pad(vbs), pad(gks), pad(gbs), state_gla[0],
                          w["g_gla_out"], chunk=s_chunk, tb=s_chunk, rows=math.gcd(bs, 8))
    y_sample = _post_mixer(h1s, att_s.reshape(ns, ATT_W).astype(BF16),
                           gla_s[:, :t_new].reshape(ns, GV_W), p_sample[0].reshape(ns, -1),
                           w, ns).reshape(bs, t_new, d)

    return (y_prompt, y_sample, k_win_prompt, v_win_prompt, state_p[None],
            k_win_sample[None], v_win_sample[None], state_s[None])
```

```python
import functools
import math

import jax
import jax.numpy as jnp
import numpy as np
from jax import lax
from jax.experimental import pallas as pl
from jax.experimental.pallas import tpu as pltpu

F32 = jnp.float32
BF16 = jnp.bfloat16

H_A = 8
HD_A = 64
ATT_W = H_A * HD_A
DILATED_BRANCHES = ((128, 1), (512, 4), (2048, 16))
ROT_DIM = HD_A // 4
ROPE_THETA = 500000.0
PAST_LEN = 8192
H_B = 4
DK_B = 64
DV_B = 128
GK_W = H_B * DK_B
GV_W = H_B * DV_B
GATE_RANK = 16
GATE_NORM = 16.0
GLA_CHUNK = 64
EPS = 1e-6
NEG = -1e30
LOG2_E = math.log2(math.e)

LANES = 128
MXU_TILE = 256
FFN_CHUNKS = 4
ROW_GROUP = 256
VMEM_LIMIT_BYTES = 56 * 1024 * 1024
ATT_BLK = 128
ATT_UNROLL = 16


def _rms(x, g):
    return x * lax.rsqrt(jnp.mean(x * x, axis=-1, keepdims=True) + EPS) * g


def _dot(a, b):
    return jnp.dot(a, b, preferred_element_type=F32)


def _dot_nt(a, b):
    return lax.dot_general(a, b, (((1,), (1,)), ((), ())), preferred_element_type=F32)


def _dot_tn(a, b):
    return lax.dot_general(a, b, (((0,), (0,)), ((), ())), preferred_element_type=F32)


def _row_groups(ref_rows):
    n = ref_rows // ROW_GROUP if ref_rows % ROW_GROUP == 0 else 1
    step = ref_rows // n
    return [slice(i * step, (i + 1) * step) for i in range(n)]


def _swiglu_half_step(xs, g_ref, wgu_ref, wdn_ref):
    d_ff = wdn_ref.shape[0]
    xns = [_rms(x, g_ref[...]).astype(BF16) for x in xs]
    n_tiles = -(-d_ff // MXU_TILE)
    n_chunk = min(FFN_CHUNKS, n_tiles)
    edges = [min(d_ff, (n_tiles * c // n_chunk) * MXU_TILE) for c in range(n_chunk + 1)]
    accs = [None] * len(xs)
    for lo, hi in zip(edges[:-1], edges[1:]):
        for i, xn in enumerate(xns):
            gate = _dot(xn, wgu_ref[:, lo:hi])
            up = _dot(xn, wgu_ref[:, d_ff + lo:d_ff + hi])
            act = (gate * jax.nn.sigmoid(gate) * up).astype(BF16)
            part = _dot(act, wdn_ref[lo:hi, :])
            accs[i] = part if accs[i] is None else accs[i] + part
    return [x + 0.5 * acc for x, acc in zip(xs, accs)]


def _pre_mixer_kernel(x_ref, g1_ref, wgu_ref, wdn_ref, gm_ref, win_ref, wa2_ref, ba_ref,
                      rc_ref, rsa_ref, rsb_ref,
                      h1_ref, qa_ref, ka_ref, va_ref, qb_ref, kb_ref, vb_ref, gk_ref, gb_ref,
                      kat_ref, vat_ref):
    groups = _row_groups(x_ref.shape[0])
    h1s = _swiglu_half_step([x_ref[sl, :] for sl in groups], g1_ref, wgu_ref, wdn_ref)
    for sl, h1 in zip(groups, h1s):
        h1_ref[sl, :] = h1
    xms = [_rms(h1, gm_ref[...]).astype(BF16) for h1 in h1s]

    def rope(t, sl, scale):
        rc, rsa, rsb = rc_ref[sl, :], rsa_ref[sl, :], rsb_ref[sl, :]
        cols = []
        for c in range(ATT_W // LANES):
            blk = t[:, c * LANES:(c + 1) * LANES]
            rot = (blk * rc + pltpu.roll(blk, ROT_DIM // 2, 1) * rsa
                   + pltpu.roll(blk, LANES - ROT_DIM // 2, 1) * rsb)
            cols.append(rot * scale if scale != 1.0 else rot)
        return jnp.concatenate(cols, axis=1)

    off = 3 * ATT_W
    for sl, xm in zip(groups, xms):
        def proj(lo, width, xm=xm):
            return _dot_nt(xm, win_ref[lo:lo + width, :])

        qa_ref[sl, :] = rope(proj(0, ATT_W), sl, HD_A ** -0.5 * LOG2_E)
        ka = rope(proj(ATT_W, ATT_W), sl, 1.0)
        va = proj(2 * ATT_W, ATT_W)
        ka_ref[sl, :] = ka
        va_ref[sl, :] = va
        kat_ref[0, :, sl] = ka.T
        vat_ref[0, :, sl] = va.T
        qb_ref[sl, :] = proj(off, GK_W) * (DK_B ** -0.5)
        kb_ref[sl, :] = proj(off + GK_W, GK_W)
        vb_ref[sl, :] = proj(off + 2 * GK_W, GV_W)
        a1 = proj(off + 2 * GK_W + GV_W, GATE_RANK)
        gb_ref[sl, :] = proj(off + 2 * GK_W + GV_W + GATE_RANK, GV_W)
        z = _dot(a1.astype(BF16), wa2_ref[...]) + ba_ref[...]
        gk_ref[sl, :] = (jnp.minimum(z, 0.0) - jnp.log1p(jnp.exp(-jnp.abs(z)))) * (1.0 / GATE_NORM)


def _pre_mixer(x, w, rope_tabs, tm, rope_period_blocks):
    n, d = x.shape
    grid = (n // tm,)

    def tok(width):
        return pl.BlockSpec((tm, width), lambda i: (i, 0))

    def full(a):
        return pl.BlockSpec(a.shape, lambda i: (0,) * a.ndim)

    rope_spec = pl.BlockSpec((tm, LANES), lambda i: (i % rope_period_blocks, 0))
    weights = (w["g_ffn1"], w["w_ffn1_gu"], w["w_ffn1_down"], w["g_mix"], w["w_in"],
               w["w_gla_a2"], w["b_gla_a"])
    out_widths = (d, ATT_W, ATT_W, ATT_W, GK_W, GK_W, GV_W, GK_W, GV_W)
    tps = tm * rope_period_blocks
    tok_last = pl.BlockSpec((1, ATT_W, tm),
                            lambda i: (i // rope_period_blocks, 0, i % rope_period_blocks))
    tok_last_shape = jax.ShapeDtypeStruct((n // tps, ATT_W, tps), F32)
    return pl.pallas_call(
        _pre_mixer_kernel,
        grid=grid,
        in_specs=[tok(d)] + [full(a) for a in weights] + [rope_spec] * 3,
        out_specs=[tok(wd) for wd in out_widths] + [tok_last] * 2,
        out_shape=[jax.ShapeDtypeStruct((n, wd), F32) for wd in out_widths] + [tok_last_shape] * 2,
        compiler_params=pltpu.CompilerParams(
            dimension_semantics=("parallel",), vmem_limit_bytes=VMEM_LIMIT_BYTES),
        name="pre_mixer",
    )(x, *weights, *rope_tabs)


def _post_mixer_kernel(h1_ref, att_ref, gla_ref, p_ref, wout_ref, g2_ref, wgu_ref, wdn_ref,
                       gp_ref, wpg_ref, wpp_ref, gf_ref, y_ref):
    groups = _row_groups(h1_ref.shape[0])
    h2s = [h1_ref[sl, :] + _dot(att_ref[sl, :], wout_ref[:ATT_W, :])
           + _dot(gla_ref[sl, :], wout_ref[ATT_W:, :]) for sl in groups]
    h3s = _swiglu_half_step(h2s, g2_ref, wgu_ref, wdn_ref)
    for sl, h3 in zip(groups, h3s):
        gate = jax.nn.sigmoid(_dot(_rms(h3, gp_ref[...]).astype(BF16), wpg_ref[...]))
        h4 = h3 + gate * _dot(p_ref[sl, :].astype(BF16), wpp_ref[...])
        y_ref[sl, :] = _rms(h4, gf_ref[...])


def _post_mixer(h1, att, gla, p, w, tm):
    n, d = h1.shape
    grid = (n // tm,)

    def tok(width):
        return pl.BlockSpec((tm, width), lambda i: (i, 0))

    def full(a):
        return pl.BlockSpec(a.shape, lambda i: (0,) * a.ndim)

    weights = (w["w_out"], w["g_ffn2"], w["w_ffn2_gu"], w["w_ffn2_down"], w["g_ple"],
               w["w_ple_gate"], w["w_ple_proj"], w["g_final"])
    return pl.pallas_call(
        _post_mixer_kernel,
        grid=grid,
        in_specs=[tok(d), tok(ATT_W), tok(GV_W), tok(p.shape[1])] + [full(a) for a in weights],
        out_specs=tok(d),
        out_shape=jax.ShapeDtypeStruct((n, d), F32),
        compiler_params=pltpu.CompilerParams(
            dimension_semantics=("parallel",), vmem_limit_bytes=VMEM_LIMIT_BYTES),
        name="post_mixer",
    )(h1, att, gla, p, *weights)


def _attn_prompt_kernel(q_ref, k_ref, v_ref, o_ref, oacc_ref, max_ref, den_ref):
    seq = q_ref.shape[1]
    qv, kv, vv = q_ref.at[0], k_ref.at[0], v_ref.at[0]
    blk = ATT_BLK
    lane = lax.broadcasted_iota(jnp.int32, (1, LANES), 1)
    head0 = lane < HD_A
    qi = lax.broadcasted_iota(jnp.int32, (blk, 2 * blk), 0)
    kj = lax.broadcasted_iota(jnp.int32, (blk, 2 * blk), 1)
    in_cur = (kj >= blk) & (kj - blk <= qi)
    in_prev = (kj < blk) & (qi <= kj)
    bias_first = jnp.where(in_cur, 0.0, NEG)
    bias_rest = jnp.where(in_cur | in_prev, 0.0, NEG)

    for bi, (window, dil) in enumerate(DILATED_BRANCHES):
        assert window // dil == blk
        nb = seq // (dil * blk)

        def unit(idx, dil=dil, nb=nb, bi=bi):
            r = idx // nb
            n = idx % nb
            start = r + n * (blk * dil)
            pstart = jnp.maximum(start - blk * dil, r)
            if dil == 1:
                start = pl.multiple_of(start, blk)
                pstart = pl.multiple_of(pstart, blk)

            def rows(ref, s0):
                if dil == 1:
                    return ref[pl.ds(s0, blk), :]
                return ref[pl.ds(s0, blk, stride=dil), :]

            qb = rows(qv, start)
            if nb == 1:
                none = jnp.zeros((blk, LANES), F32)
                kcat = jnp.concatenate([none, rows(kv, start)], axis=0).astype(BF16)
                vcat = jnp.concatenate([none, rows(vv, start)], axis=0).astype(BF16)
                bias = bias_first
            else:
                kcat = jnp.concatenate([rows(kv, pstart), rows(kv, start)], axis=0).astype(BF16)
                vcat = jnp.concatenate([rows(vv, pstart), rows(vv, start)], axis=0).astype(BF16)
                bias = jnp.where(n > 0, bias_rest, bias_first)
            outs, maxes, dens = [], [], []
            for hmask in (head0, ~head0):
                qh = jnp.where(hmask, qb, 0.0).astype(BF16)
                s = _dot_nt(qh, kcat) + bias
                m = jnp.max(s, axis=-1, keepdims=True)
                p = jnp.exp2(s - m)
                maxes.append(m)
                dens.append(jnp.sum(p, axis=-1, keepdims=True))
                outs.append(_dot(p.astype(BF16), vcat))
            stats = (jnp.where(head0, outs[0], outs[1]), jnp.where(head0, maxes[0], maxes[1]),
                     jnp.where(head0, dens[0], dens[1]))
            for ref, val in zip((oacc_ref, max_ref, den_ref), stats):
                if dil == 1:
                    ref[bi, pl.ds(start, blk), :] = val
                else:
                    ref.at[bi][pl.ds(start, blk, stride=dil), :] = val

        def body(it, carry, unit=unit):
            for u in range(ATT_UNROLL):
                unit(it * ATT_UNROLL + u)
            return carry

        lax.fori_loop(0, seq // (blk * ATT_UNROLL), body, 0)

    rows_per = 256
    nbr = len(DILATED_BRANCHES)
    for c in range(seq // rows_per):
        sl = slice(c * rows_per, (c + 1) * rows_per)
        ms = [max_ref[b, sl, :] for b in range(nbr)]
        top = functools.reduce(jnp.maximum, ms)
        es = [jnp.exp2(m - top) for m in ms]
        num = sum(e * oacc_ref[b, sl, :] for b, e in enumerate(es))
        tot = sum(e * den_ref[b, sl, :] for b, e in enumerate(es))
        o_ref[0, sl, :] = (num / tot).astype(o_ref.dtype)


def _attn_prompt(q, k, v):
    b, seq, width = q.shape
    spec = pl.BlockSpec((1, seq, LANES), lambda i, j: (i, 0, j))
    nbr = len(DILATED_BRANCHES)
    return pl.pallas_call(
        _attn_prompt_kernel,
        grid=(b, width // LANES),
        in_specs=[spec] * 3,
        out_specs=spec,
        out_shape=jax.ShapeDtypeStruct((b, seq, width), BF16),
        scratch_shapes=[pltpu.VMEM((nbr, seq, LANES), F32)] * 3,
        compiler_params=pltpu.CompilerParams(
            dimension_semantics=("parallel", "parallel"), vmem_limit_bytes=VMEM_LIMIT_BYTES),
        name="attn_prompt",
    )(q, k, v)


T_PAD = 8


def _attn_sample_kernel(q_ref, knt_ref, vnt_ref, kt_ref, vt_ref, att_ref, ko_ref, vo_ref,
                        knr_ref, vnr_ref, *, t_new):
    n_heads, dh, w = kt_ref.shape[1], kt_ref.shape[2], kt_ref.shape[3]
    row = pl.program_id(0)

    @pl.when(row == 0)
    def _():
        for b in range(knr_ref.shape[0]):
            shift = LANES - t_new - b * t_new
            knr_ref[b] = pltpu.roll(knt_ref[0], shift, 1)
            vnr_ref[b] = pltpu.roll(vnt_ref[0], shift, 1)
    tq = lax.broadcasted_iota(jnp.int32, (T_PAD, w), 0)
    tok = lax.broadcasted_iota(jnp.int32, (T_PAD, w), 1)
    tqn = lax.broadcasted_iota(jnp.int32, (T_PAD, LANES), 0)
    un = lax.broadcasted_iota(jnp.int32, (T_PAD, LANES), 1) - (LANES - t_new)
    lane_new = lax.broadcasted_iota(jnp.int32, (1, LANES), 1) >= LANES - t_new
    cache_masks, new_masks = [], []
    for window, dil in DILATED_BRANCHES:
        assert w % dil == 0 and dil & (dil - 1) == 0
        cache_masks.append((tq < t_new) & (tok >= w + tq - window)
                           & (((tok - tq) & (dil - 1)) == 0))
        new_masks.append((tqn < t_new) & (un >= 0) & (un <= tqn)
                         & (((tqn - un) & (dil - 1)) == 0))

    for h in range(n_heads):
        q = q_ref[0, h].astype(BF16)
        kt, vt = kt_ref[0, h], vt_ref[0, h]
        knp = knr_ref[row, h * dh:(h + 1) * dh, :]
        vnp = vnr_ref[row, h * dh:(h + 1) * dh, :]
        sc = _dot(q, kt.astype(BF16))
        sn = _dot(q, knp.astype(BF16))
        pcs, pns, dens, lses = [], [], [], []
        for cm, nm in zip(cache_masks, new_masks):
            scm = jnp.where(cm, sc, NEG)
            snm = jnp.where(nm, sn, NEG)
            m = jnp.maximum(jnp.max(scm, axis=-1, keepdims=True), jnp.max(snm, axis=-1, keepdims=True))
            pc, pn = jnp.exp2(scm - m), jnp.exp2(snm - m)
            den = jnp.sum(pc, axis=-1, keepdims=True) + jnp.sum(pn, axis=-1, keepdims=True)
            pcs.append(pc)
            pns.append(pn)
            dens.append(den)
            lses.append(m + jnp.log2(den))
        o_all = (_dot_nt(jnp.concatenate(pcs, axis=0).astype(BF16), vt.astype(BF16))
                 + _dot_nt(jnp.concatenate(pns, axis=0).astype(BF16), vnp.astype(BF16)))
        mx = jnp.maximum(jnp.maximum(lses[0], lses[1]), lses[2])
        num, tot = 0.0, 0.0
        for bi in range(len(DILATED_BRANCHES)):
            wgt = jnp.exp2(lses[bi] - mx)
            num = num + wgt * (o_all[bi * T_PAD:(bi + 1) * T_PAD] / dens[bi])
            tot = tot + wgt
        att_ref[0, h] = num / tot
        for src, newp, dst in ((kt, knp, ko_ref), (vt, vnp, vo_ref)):
            rolled = pltpu.roll(src, w - t_new, 1)
            dst[0, h, :, :w - LANES] = rolled[:, :w - LANES]
            dst[0, h, :, w - LANES:] = jnp.where(lane_new, newp, rolled[:, w - LANES:])


def _attn_sample(q4, knt, vnt, kt, vt, t_new):
    b, h, _, dh = q4.shape
    w = kt.shape[3]
    assert knt.shape == (1, h * dh, LANES) and b * t_new == LANES
    q_spec = pl.BlockSpec((1, h, T_PAD, dh), lambda i: (i, 0, 0, 0))
    new_spec = pl.BlockSpec(knt.shape, lambda i: (0, 0, 0))
    cache_spec = pl.BlockSpec((1, h, dh, w), lambda i: (i, 0, 0, 0))
    return pl.pallas_call(
        functools.partial(_attn_sample_kernel, t_new=t_new),
        grid=(b,),
        in_specs=[q_spec, new_spec, new_spec, cache_spec, cache_spec],
        out_specs=[q_spec, cache_spec, cache_spec],
        out_shape=[jax.ShapeDtypeStruct(q4.shape, F32), jax.ShapeDtypeStruct(kt.shape, F32),
                   jax.ShapeDtypeStruct(vt.shape, F32)],
        scratch_shapes=[pltpu.VMEM((b, h * dh, LANES), F32), pltpu.VMEM((b, h * dh, LANES), F32)],
        compiler_params=pltpu.CompilerParams(
            dimension_semantics=("arbitrary",), vmem_limit_bytes=VMEM_LIMIT_BYTES),
        name="attn_sample",
    )(q4, knt, vnt, kt, vt)


def _gla_decay_matrix(chunk):
    t = np.arange(chunk)[:, None]
    u = np.arange(chunk)[None, :]
    blocks = [u <= t, u > t]
    s = chunk // 2
    while s >= 1:
        mid = (t // (2 * s)) * (2 * s) + s
        upper = (t >= mid) & (u >= mid) & (u <= t)
        lower = (t < mid) & (u > t) & (u < mid)
        blocks.append(upper | lower)
        s //= 2
    return np.concatenate(blocks, axis=0).astype(np.float32)


def _gla_kernel(q_ref, k_ref, v_ref, g_ref, gb_ref, s0_ref, m_ref, gout_ref,
                o_ref, sfin_ref, st_ref, *, chunk):
    j = pl.program_id(1)
    n_lvl = int(math.log2(chunk))
    n_rows, tb = q_ref.shape[0], q_ref.shape[1]
    nc = tb // chunk
    hk = lax.broadcasted_iota(jnp.int32, (1, GK_W), 1) // DK_B
    hv = lax.broadcasted_iota(jnp.int32, (1, GV_W), 1) // DV_B
    hv_rows = lax.broadcasted_iota(jnp.int32, (GV_W, 1), 0) // DV_B
    state_mask = hv_rows == hk

    @pl.when(j == 0)
    def _():
        for r in range(n_rows):
            s0t = s0_ref[r].reshape(GK_W, DV_B).T
            st_ref[r] = jnp.where(state_mask, jnp.concatenate([s0t] * H_B, axis=0), 0.0)

    pi = lax.broadcasted_iota(jnp.int32, (chunk, H_B * chunk), 0)
    pj = lax.broadcasted_iota(jnp.int32, (chunk, H_B * chunk), 1) % chunk
    pair_masks = []
    s = chunk // 2
    while s >= 1:
        same = (pi // (2 * s)) == (pj // (2 * s))
        pair_masks.append(same & (pi % (2 * s) >= s) & (pj % (2 * s) < s))
        s //= 2
    pair_masks.append(pi == pj)
    a_head = lax.broadcasted_iota(jnp.int32, (H_B * chunk, 1), 0) // chunk
    bd_k = jnp.where(a_head == hk, 1.0, 0.0).astype(BF16)
    bd_v = jnp.where(a_head == hv, 1.0, 0.0).astype(BF16)

    mdec2 = m_ref[...]
    gout = gout_ref[...]
    units = [(r, slice(c * chunk, (c + 1) * chunk)) for r in range(n_rows) for c in range(nc)]
    qs = [q_ref[r, sl, :] for r, sl in units]
    ks = [k_ref[r, sl, :] for r, sl in units]
    xs = []
    for r, sl in units:
        g = g_ref[r, sl, :]
        g_hi = g.astype(BF16)
        g_lo = (g - g_hi.astype(F32)).astype(BF16)
        xs.append(jnp.exp(_dot(mdec2, jnp.concatenate([g_hi, g_lo], axis=0))))
    a_alls = [jnp.zeros((chunk, H_B * chunk), F32)] * len(units)
    for lvl in range(n_lvl + 1):
        for u in range(len(units)):
            if lvl < n_lvl:
                xl = xs[u][(2 + lvl) * chunk:(3 + lvl) * chunk]
                ql, kl = (qs[u] * xl).astype(BF16), (ks[u] * xl).astype(BF16)
            else:
                ql, kl = qs[u].astype(BF16), ks[u].astype(BF16)
            kl_bd = jnp.concatenate([kl] * H_B, axis=0) * bd_k
            a_alls[u] = jnp.where(pair_masks[lvl], _dot_nt(ql, kl_bd), a_alls[u])
    o_intra, q_dec, upds, decs = [], [], [], []
    for u, (r, sl) in enumerate(units):
        vb = v_ref[r, sl, :].astype(BF16)
        v_bd = jnp.concatenate([vb] * H_B, axis=0) * bd_v
        o_intra.append(_dot(a_alls[u].astype(BF16), v_bd))
        x_cum, x_rem = xs[u][0:chunk], xs[u][chunk:2 * chunk]
        q_dec.append((qs[u] * x_cum).astype(BF16))
        upds.append(jnp.where(state_mask, _dot_tn(vb, (ks[u] * x_rem).astype(BF16)), 0.0))
        decs.append(x_cum[chunk - 1:chunk, :])
    for r in range(n_rows):
        st = st_ref[r]
        for c in range(nc):
            u = r * nc + c
            sl = units[u][1]
            o = _dot_nt(q_dec[u], st.astype(BF16)) + o_intra[u]
            st = st * decs[u] + upds[u]
            gb = gb_ref[r, sl, :]
            for h in range(H_B):
                hs = slice(h * DV_B, (h + 1) * DV_B)
                gh = gb[:, hs]
                o_ref[r, sl, hs] = (_rms(o[:, hs], gout) * (gh * jax.nn.sigmoid(gh))).astype(o_ref.dtype)
        st_ref[r] = st

    @pl.when(j == pl.num_programs(1) - 1)
    def _():
        for r in range(n_rows):
            st = st_ref[r]
            s_t = st[0:DV_B]
            for h in range(1, H_B):
                s_t = s_t + st[h * DV_B:(h + 1) * DV_B]
            sfin_ref[r] = s_t.T.reshape(H_B, DK_B, DV_B)


def _gla(q, k, v, g, gb, s0, g_out, chunk, tb, rows):
    b, t, _ = q.shape
    m = _gla_decay_matrix(chunk)
    mdec = jnp.asarray(np.concatenate([m, m], axis=1), dtype=BF16)

    def tok(width):
        return pl.BlockSpec((rows, tb, width), lambda i, j: (i, j, 0))

    state_spec = pl.BlockSpec((rows, H_B, DK_B, DV_B), lambda i, j: (i, 0, 0, 0))
    return pl.pallas_call(
        functools.partial(_gla_kernel, chunk=chunk),
        grid=(b // rows, t // tb),
        in_specs=[tok(GK_W), tok(GK_W), tok(GV_W), tok(GK_W), tok(GV_W), state_spec,
                  pl.BlockSpec(mdec.shape, lambda i, j: (0, 0)),
                  pl.BlockSpec(g_out.shape, lambda i, j: (0, 0))],
        out_specs=[tok(GV_W), state_spec],
        out_shape=[jax.ShapeDtypeStruct((b, t, GV_W), BF16),
                   jax.ShapeDtypeStruct((b, H_B, DK_B, DV_B), F32)],
        scratch_shapes=[pltpu.VMEM((rows, GV_W, GK_W), F32)],
        compiler_params=pltpu.CompilerParams(
            dimension_semantics=("parallel", "arbitrary"), vmem_limit_bytes=VMEM_LIMIT_BYTES),
        name="gla",
    )(q, k, v, g, gb, s0, mdec, g_out)


def _rope_tables(pos):
    half = ROT_DIM // 2
    inv_freq = ROPE_THETA ** (-jnp.arange(half, dtype=F32) * (2.0 / ROT_DIM))
    ang = pos.astype(F32)[:, None] * inv_freq[None, :]
    cos, sin = jnp.cos(ang), jnp.sin(ang)
    n = pos.shape[0]
    ones = jnp.ones((n, HD_A - ROT_DIM), F32)
    zeros = jnp.zeros((n, HD_A - ROT_DIM), F32)
    zh = jnp.zeros((n, half), F32)
    c = jnp.concatenate([cos, cos, ones], axis=1)
    sa = jnp.concatenate([zh, sin, zeros], axis=1)
    sb = jnp.concatenate([-sin, zh, zeros], axis=1)
    rep = LANES // HD_A
    return tuple(jnp.tile(tab, (1, rep)) for tab in (c, sa, sb))


def _prep_weights(g_ffn1, w_ffn1_gu, w_ffn1_down, g_mix, w_in, w_gla_a2, b_gla_a, g_gla_out, w_out,
                  g_ffn2, w_ffn2_gu, w_ffn2_down, g_ple, w_ple_gate, w_ple_proj, g_final):
    win = w_in[0].T
    wa2 = w_gla_a2[0]
    return {
        "g_ffn1": g_ffn1, "w_ffn1_gu": w_ffn1_gu[0].astype(BF16),
        "w_ffn1_down": w_ffn1_down[0].astype(BF16), "g_mix": g_mix,
        "w_in": win.astype(BF16), "w_gla_a2": wa2.astype(BF16), "b_gla_a": b_gla_a,
        "g_gla_out": g_gla_out, "w_out": w_out[0].astype(BF16), "g_ffn2": g_ffn2,
        "w_ffn2_gu": w_ffn2_gu[0].astype(BF16), "w_ffn2_down": w_ffn2_down[0].astype(BF16),
        "g_ple": g_ple, "w_ple_gate": w_ple_gate[0].astype(BF16),
        "w_ple_proj": w_ple_proj[0].astype(BF16), "g_final": g_final.reshape(1, -1),
    }


def kernel(x_prompt, x_sample, cache_k_win, cache_v_win, state_gla, p_prompt, p_sample, g_ffn1, w_ffn1_gu, w_ffn1_down, g_mix, w_in, w_gla_a2, b_gla_a, g_gla_out, w_out, g_ffn2, w_ffn2_gu, w_ffn2_down, g_ple, w_ple_gate, w_ple_proj, g_final):
    assert w_in.shape[0] == 1, "single-layer step"
    w = _prep_weights(g_ffn1, w_ffn1_gu, w_ffn1_down, g_mix, w_in, w_gla_a2, b_gla_a, g_gla_out,
                      w_out, g_ffn2, w_ffn2_gu, w_ffn2_down, g_ple, w_ple_gate, w_ple_proj, g_final)
    bp, seq, d = x_prompt.shape
    bs, t_new, _ = x_sample.shape
    win_buf = cache_k_win.shape[2]
    assert seq == DILATED_BRANCHES[-1][0] == win_buf

    tm = 512
    rope_p = _rope_tables(jnp.arange(seq, dtype=jnp.int32))
    (h1, qa, ka, va, qb, kb, vb, gk, gb, kat, vat) = _pre_mixer(
        x_prompt.reshape(bp * seq, d), w, rope_p, tm, seq // tm)
    r3 = lambda a: a.reshape(bp, seq, a.shape[-1])
    att_p = _attn_prompt(r3(qa), r3(ka), r3(va))
    gla_p, state_p = _gla(r3(qb), r3(kb), r3(vb), r3(gk), r3(gb),
                          jnp.zeros((bp, H_B, DK_B, DV_B), F32), w["g_gla_out"],
                          chunk=GLA_CHUNK, tb=16 * GLA_CHUNK, rows=1)
    y_prompt = _post_mixer(h1, att_p.reshape(bp * seq, ATT_W), gla_p.reshape(bp * seq, GV_W),
                           p_prompt[0].reshape(bp * seq, -1), w, 2 * tm).reshape(bp, seq, d)
    win5 = lambda a: a.reshape(bp, H_A, HD_A, seq).transpose(0, 3, 1, 2)[None]
    k_win_prompt, v_win_prompt = win5(kat), win5(vat)

    ns = bs * t_new
    rope_s = _rope_tables(jnp.tile(PAST_LEN + jnp.arange(t_new, dtype=jnp.int32), bs))
    (h1s, qas, _, _, qbs, kbs, vbs, gks, gbs, kats, vats) = _pre_mixer(
        x_sample.reshape(ns, d), w, rope_s, ns, 1)
    q4 = jnp.pad(qas.reshape(bs, t_new, H_A, HD_A).transpose(0, 2, 1, 3),
                 ((0, 0), (0, 0), (0, T_PAD - t_new), (0, 0)))
    tok_last = lambda c: c[0].transpose(0, 2, 3, 1)
    att4, kto, vto = _attn_sample(q4, kats, vats, tok_last(cache_k_win), tok_last(cache_v_win),
                                  t_new)
    att_s = att4[:, :, :t_new].transpose(0, 2, 1, 3)
    k_win_sample = kto.transpose(0, 3, 1, 2)
    v_win_sample = vto.transpose(0, 3, 1, 2)
    s_chunk = 16
    pad = lambda a: jnp.pad(a.reshape(bs, t_new, a.shape[-1]), ((0, 0), (0, s_chunk - t_new), (0, 0)))
    gla_s, state_s = _gla(pad(qbs), pad(kbs), pad(vbs), pad(gks), pad(gbs), state_gla[0],
                          w["g_gla_out"], chunk=s_chunk, tb=s_chunk, rows=math.gcd(bs, 8))
    y_sample = _post_mixer(h1s, att_s.reshape(ns, ATT_W).astype(BF16),
                           gla_s[:, :t_new].reshape(ns, GV_W), p_sample[0].reshape(ns, -1),
                           w, ns).reshape(bs, t_new, d)

    return (y_prompt, y_sample, k_win_prompt, v_win_prompt, state_p[None],
            k_win_sample[None], v_win_sample[None], state_s[None])
```
